```python
import math
import jax, jax.numpy as jnp
from jax import lax
import numpy as np

D_MODEL = 1024
BATCH = 2
SEQ = 16384
DEPTH = 1

EPS = 1e-6
CONV_WIDTH = D_MODEL
CONV_TAPS = 31
DIFF_HEADS = 8
DIFF_HEAD_DIM = 64
DIFF_V_DIM = 2 * DIFF_HEAD_DIM
QK_WIDTH = DIFF_HEADS * 2 * DIFF_HEAD_DIM
ATTN_WIDTH = DIFF_HEADS * DIFF_V_DIM
ROPE_THETA = 500000.0
ROPE_DIM = DIFF_HEAD_DIM // 4
Q_BLOCK = 128
IN_SIZES = (CONV_WIDTH, CONV_WIDTH, QK_WIDTH, QK_WIDTH, ATTN_WIDTH, D_MODEL, D_MODEL)
IN_WIDTH = 2 * CONV_WIDTH + 2 * QK_WIDTH + ATTN_WIDTH + 2 * D_MODEL
N_EXPERTS = 64
N_GROUPS = 8
TOPK_GROUPS = 4
TOP_K = 8
EXPERT_DIM = 256
SHARED_DIM = 256
ROUTED_SCALE = 2.5
MOE_CHUNK = 128

kernel_name = 'hybrid_conv_diffattn_moe_block'


def rms_norm(x, g):
    x32 = x.astype(jnp.float32)
    y = x32 * lax.rsqrt(jnp.mean(x32 * x32, axis=-1, keepdims=True) + EPS)
    return y.astype(x.dtype) * g


def layer_norm(x, g, b):
    x32 = x.astype(jnp.float32)
    mu = jnp.mean(x32, axis=-1, keepdims=True)
    xc = x32 - mu
    y = xc * lax.rsqrt(jnp.mean(xc * xc, axis=-1, keepdims=True) + EPS)
    return y.astype(x.dtype) * g + b


def rope(x, cos, sin):
    half = ROPE_DIM // 2
    x1, x2, xp = x[..., :half], x[..., half:ROPE_DIM], x[..., ROPE_DIM:]
    return jnp.concatenate([x1 * cos - x2 * sin, x2 * cos + x1 * sin, xp], axis=-1)


def conformer_conv(a, b, w_dw, b_dw, ln_g, ln_b, w_pw2, b_pw2):
    u = a * jax.nn.sigmoid(b)
    u = lax.conv_general_dilated(
        u, w_dw[:, None, :], window_strides=(1,), padding=[(CONV_TAPS - 1, 0)],
        dimension_numbers=('NWC', 'WIO', 'NWC'), feature_group_count=CONV_WIDTH) + b_dw
    u = jax.nn.silu(layer_norm(u, ln_g, ln_b))
    return u @ w_pw2 + b_pw2


def diff_attention(q, k, v, cos, sin, q_g, k_g, lq1, lk1, lq2, lk2, subln_g, lambda_init):
    B, S, _ = q.shape
    H, d = DIFF_HEADS, DIFF_HEAD_DIM
    q = q.reshape(B, S, H, 2, d).transpose(0, 2, 3, 1, 4)
    k = k.reshape(B, S, H, 2, d).transpose(0, 2, 3, 1, 4)
    v = v.reshape(B, S, H, DIFF_V_DIM).transpose(0, 2, 1, 3)
    q = rope(rms_norm(q, q_g), cos, sin)
    k = rope(rms_norm(k, k_g), cos, sin)
    f32 = jnp.float32
    lam = (jnp.exp(jnp.sum(lq1.astype(f32) * lk1.astype(f32)))
           - jnp.exp(jnp.sum(lq2.astype(f32) * lk2.astype(f32))) + lambda_init)
    nb = S // Q_BLOCK
    q_blocks = jnp.moveaxis(q.reshape(B, H, 2, nb, Q_BLOCK, d), 3, 0)
    starts = jnp.arange(nb, dtype=jnp.int32) * Q_BLOCK
    k_idx = jnp.arange(S, dtype=jnp.int32)
    scale = d ** -0.5

    def block(args):
        qb, start = args
        s = jnp.einsum('bhmqd,bhmkd->bhmqk', qb, k).astype(f32) * scale
        causal = k_idx[None, :] <= (start + jnp.arange(Q_BLOCK, dtype=jnp.int32))[:, None]
        p = jax.nn.softmax(jnp.where(causal, s, -jnp.inf), axis=-1)
        a = p[:, :, 0] - lam * p[:, :, 1]
        return jnp.einsum('bhqk,bhkv->bhqv', a.astype(v.dtype), v)

    o = lax.map(block, (q_blocks, starts))
    o = jnp.moveaxis(o, 0, 2).reshape(B, H, S, DIFF_V_DIM)
    o = rms_norm(o, subln_g) * (1.0 - lambda_init)
    return o.transpose(0, 2, 1, 3).reshape(B, S, ATTN_WIDTH)


def moe_ffn(h, w_router, router_bias, w_gu, w_down, w_sh_gu, w_sh_down):
    B, S, D = h.shape
    f32 = jnp.float32
    t = h.reshape(B * S, D)
    scores = jax.nn.sigmoid((t @ w_router).astype(f32))
    choice = scores + router_bias.astype(f32)
    per_group = N_EXPERTS // N_GROUPS
    group_score = lax.top_k(choice.reshape(-1, N_GROUPS, per_group), 2)[0].sum(-1)
    _, gidx = lax.top_k(group_score, TOPK_GROUPS)
    gmask = jax.nn.one_hot(gidx, N_GROUPS, dtype=f32).sum(-2)
    emask = jnp.repeat(gmask, per_group, axis=-1) > 0
    _, eidx = lax.top_k(jnp.where(emask, choice, -jnp.inf), TOP_K)
    w = jnp.take_along_axis(scores, eidx, axis=-1)
    w = w / jnp.sum(w, axis=-1, keepdims=True) * ROUTED_SCALE
    gates = jnp.einsum('tk,tke->te', w, jax.nn.one_hot(eidx, N_EXPERTS, dtype=f32)).astype(h.dtype)
    nc = t.shape[0] // MOE_CHUNK

    def chunk(args):
        tc, gc = args
        g, u = jnp.split(jnp.einsum('td,edf->tef', tc, w_gu), 2, axis=-1)
        return jnp.einsum('tef,efd->td', jax.nn.silu(g) * u * gc[..., None], w_down)

    routed = lax.map(chunk, (t.reshape(nc, MOE_CHUNK, D),
                             gates.reshape(nc, MOE_CHUNK, N_EXPERTS))).reshape(B * S, D)
    sg, su = jnp.split(t @ w_sh_gu, 2, axis=-1)
    shared = (jax.nn.silu(sg) * su) @ w_sh_down
    return (routed + shared).reshape(B, S, D)


def setup_inputs(seed: int = 0) -> dict:
    key = jax.random.key(seed)
    ks = jax.random.split(key, 32)
    L, D, C = DEPTH, D_MODEL, CONV_WIDTH
    f32 = jnp.float32

    def nrm(k, shape, scale):
        return jax.random.normal(k, shape, f32) * scale

    def gain(k, shape):
        return 1.0 + 0.05 * jax.random.normal(k, shape, f32)

    return {
        'x': nrm(ks[0], (BATCH, SEQ, D), 1.0),
        'c': nrm(ks[1], (BATCH, D), 1.0),
        'positions': jnp.broadcast_to(jnp.arange(SEQ, dtype=jnp.int32), (BATCH, SEQ)),
        'w_ada': nrm(ks[2], (L, D, 6 * D), 0.5 * D ** -0.5),
        'b_ada': nrm(ks[3], (L, 6 * D), 0.02),
        'norm1_g': gain(ks[4], (L, D)),
        'w_in': nrm(ks[5], (L, D, IN_WIDTH), D ** -0.5),
        'conv_dw': nrm(ks[6], (L, CONV_TAPS, C), CONV_TAPS ** -0.5),
        'conv_dw_b': nrm(ks[7], (L, C), 0.02),
        'conv_ln_g': gain(ks[8], (L, C)),
        'conv_ln_b': nrm(ks[9], (L, C), 0.02),
        'w_pw2': nrm(ks[10], (L, C, D), C ** -0.5),
        'b_pw2': nrm(ks[11], (L, D), 0.02),
        'q_norm_g': gain(ks[12], (L, DIFF_HEAD_DIM)),
        'k_norm_g': gain(ks[13], (L, DIFF_HEAD_DIM)),
        'lambda_q1': nrm(ks[14], (L, DIFF_HEAD_DIM), 0.1),
        'lambda_k1': nrm(ks[15], (L, DIFF_HEAD_DIM), 0.1),
        'lambda_q2': nrm(ks[16], (L, DIFF_HEAD_DIM), 0.1),
        'lambda_k2': nrm(ks[17], (L, DIFF_HEAD_DIM), 0.1),
        'subln_g': gain(ks[18], (L, DIFF_V_DIM)),
        'w_out': nrm(ks[19], (L, D, D), D ** -0.5),
        'norm2_g': gain(ks[20], (L, D)),
        'w_router': nrm(ks[21], (L, D, N_EXPERTS), D ** -0.5),
        'router_bias': nrm(ks[22], (L, N_EXPERTS), 0.01),
        'w_exp_gu': nrm(ks[23], (L, N_EXPERTS, D, 2 * EXPERT_DIM), D ** -0.5),
        'w_exp_down': nrm(ks[24], (L, N_EXPERTS, EXPERT_DIM, D), EXPERT_DIM ** -0.5),
        'w_sh_gu': nrm(ks[25], (L, D, 2 * SHARED_DIM), D ** -0.5),
        'w_sh_down': nrm(ks[26], (L, SHARED_DIM, D), SHARED_DIM ** -0.5),
    }


def reference(x, c, positions, w_ada, b_ada, norm1_g, w_in, conv_dw, conv_dw_b, conv_ln_g,
              conv_ln_b, w_pw2, b_pw2, q_norm_g, k_norm_g, lambda_q1, lambda_k1, lambda_q2,
              lambda_k2, subln_g, w_out, norm2_g, w_router, router_bias, w_exp_gu, w_exp_down,
              w_sh_gu, w_sh_down):
    inv_freq = ROPE_THETA ** (-jnp.arange(0, ROPE_DIM, 2, dtype=jnp.float32) / ROPE_DIM)
    ang = positions.astype(jnp.float32)[:, None, None, :, None] * inv_freq
    cos = jnp.cos(ang).astype(x.dtype)
    sin = jnp.sin(ang).astype(x.dtype)
    c_act = jax.nn.silu(c)
    offsets = []
    acc = 0
    for s in IN_SIZES[:-1]:
        acc += s
        offsets.append(acc)
    for l in range(DEPTH):
        lambda_init = 0.8 - 0.6 * math.exp(-0.3 * l)
        mod = c_act @ w_ada[l] + b_ada[l]
        sh1, sc1, g1, sh2, sc2, g2 = [m[:, None, :] for m in jnp.split(mod, 6, axis=-1)]
        h = rms_norm(x, norm1_g[l]) * (1 + sc1) + sh1
        proj = h @ w_in[l]
        ca, cb, q, k, v, gate_conv, gate_attn = jnp.split(proj, offsets, axis=-1)
        y_conv = conformer_conv(ca, cb, conv_dw[l], conv_dw_b[l], conv_ln_g[l], conv_ln_b[l],
                                w_pw2[l], b_pw2[l])
        y_attn = diff_attention(q, k, v, cos, sin, q_norm_g[l], k_norm_g[l], lambda_q1[l],
                                lambda_k1[l], lambda_q2[l], lambda_k2[l], subln_g[l], lambda_init)
        merged = jax.nn.sigmoid(gate_conv) * y_conv + jax.nn.sigmoid(gate_attn) * y_attn
        x = x + g1 * (merged @ w_out[l])
        h2 = rms_norm(x, norm2_g[l]) * (1 + sc2) + sh2
        x = x + g2 * moe_ffn(h2, w_router[l], router_bias[l], w_exp_gu[l], w_exp_down[l],
                             w_sh_gu[l], w_sh_down[l])
    return x
```

```python
import functools
import math

import jax
import jax.numpy as jnp
from jax import lax
from jax.experimental import pallas as pl
from jax.experimental.pallas import tpu as pltpu

F32 = jnp.float32
BF16 = jnp.bfloat16

EPS = 1e-6
CONV_TAPS = 31
CONV_HALO = 32
N_HEADS = 8
HEAD_DIM = 64
V_DIM = 2 * HEAD_DIM
ROPE_THETA = 500000.0
ROPE_DIM = HEAD_DIM // 4
N_EXPERTS = 64
N_GROUPS = 8
GROUP_SIZE = N_EXPERTS // N_GROUPS
TOPK_GROUPS = 4
TOP_K = 8
EXPERT_DIM = 256
ROUTED_SCALE = 2.5
LAMBDA_INIT = 0.8 - 0.6 * math.exp(-0.3 * 0)
LANES = 128
NEG_BIG = -1e30
VMEM_LIMIT = 56 * 1024 * 1024


def _params(sem):
    return pltpu.CompilerParams(dimension_semantics=sem, vmem_limit_bytes=VMEM_LIMIT)


def _sigmoid(x):
    return 1.0 / (1.0 + jnp.exp(-x))


def _silu(x):
    return x * _sigmoid(x)


def _adaln_kernel(c_ref, w_ref, b_ref, o_ref):
    c = c_ref[...]
    o_ref[...] = jnp.dot(_silu(c), w_ref[...], preferred_element_type=F32,
                         precision=lax.Precision.HIGHEST) + b_ref[...]


def _adaln(c, w_ada, b_ada):
    bsz, d = c.shape
    n = w_ada.shape[1]
    tn = 1536
    return pl.pallas_call(
        _adaln_kernel,
        out_shape=jax.ShapeDtypeStruct((bsz, n), F32),
        grid=(n // tn,),
        in_specs=[pl.BlockSpec((bsz, d), lambda j: (0, 0)),
                  pl.BlockSpec((d, tn), lambda j: (0, j)),
                  pl.BlockSpec((1, tn), lambda j: (0, j))],
        out_specs=pl.BlockSpec((bsz, tn), lambda j: (0, j)),
        compiler_params=_params(("parallel",)),
        name="adaln",
    )(c, w_ada, b_ada.reshape(1, n))


def _inproj_kernel(x_ref, mod_ref, g_ref, w_ref, o_ref, h_scr):
    @pl.when(pl.program_id(1) == 0)
    def _():
        x = x_ref[...]
        ms = jnp.mean(x * x, axis=-1, keepdims=True)
        mod = mod_ref[0]
        gmul = g_ref[...] * (1.0 + mod[1:2, :])
        h_scr[...] = (x * lax.rsqrt(ms + EPS) * gmul + mod[0:1, :]).astype(BF16)

    o_ref[...] = jnp.dot(h_scr[...], w_ref[...], preferred_element_type=F32).astype(BF16)


def _inproj(x2, mod8, norm_g, w_in_bf, seq, tm, tn):
    t, d = x2.shape
    n = w_in_bf.shape[1]
    tiles_per_seq = seq // tm
    return pl.pallas_call(
        _inproj_kernel,
        out_shape=jax.ShapeDtypeStruct((t, n), BF16),
        grid=(t // tm, n // tn),
        in_specs=[pl.BlockSpec((tm, d), lambda i, j: (i, 0)),
                  pl.BlockSpec((1, 8, d), lambda i, j: (i // tiles_per_seq, 0, 0)),
                  pl.BlockSpec((1, d), lambda i, j: (0, 0)),
                  pl.BlockSpec((d, tn), lambda i, j: (0, j))],
        out_specs=pl.BlockSpec((tm, tn), lambda i, j: (i, j)),
        scratch_shapes=[pltpu.VMEM((tm, d), BF16)],
        compiler_params=_params(("parallel", "arbitrary")),
        name="inproj",
    )(x2, mod8, norm_g.reshape(1, d), w_in_bf)


def _qkprep_kernel(pos_ref, q_ref, k_ref, qg_ref, kg_ref, seg_ref, qo_ref, kt_ref):
    tm = q_ref.shape[0]
    pos = pos_ref[...]
    lane = lax.broadcasted_iota(jnp.int32, (1, LANES), 1)
    in_seg = lane % HEAD_DIM
    half = ROPE_DIM // 2
    freq_idx = (in_seg % half).astype(F32)
    inv_freq = jnp.exp(freq_idx * (-2.0 * math.log(ROPE_THETA) / ROPE_DIM))
    ang = pos * inv_freq
    cos = jnp.cos(ang)
    sin = jnp.sin(ang)
    is_lo = in_seg < half
    is_hi = jnp.logical_and(in_seg >= half, in_seg < ROPE_DIM)
    c_tab = jnp.where(in_seg < ROPE_DIM, cos, 1.0)
    s_lo = jnp.where(is_lo, -sin, 0.0)
    s_hi = jnp.where(is_hi, sin, 0.0)
    seg = seg_ref[...]

    def prep(x, g):
        sq = x * x
        hi = sq.astype(BF16)
        lo = (sq - hi.astype(F32)).astype(BF16)
        ssq = (jnp.dot(hi, seg, preferred_element_type=F32)
               + jnp.dot(lo, seg, preferred_element_type=F32))
        xn = x * lax.rsqrt(ssq * (1.0 / HEAD_DIM) + EPS) * g
        up = pltpu.roll(xn, LANES - half, axis=1)
        dn = pltpu.roll(xn, half, axis=1)
        return xn * c_tab + up * s_lo + dn * s_hi

    for h in range(N_HEADS):
        cols = slice(h * LANES, (h + 1) * LANES)
        qo_ref[:, cols] = prep(q_ref[:, cols].astype(F32), qg_ref[...]).astype(BF16)
        kr = prep(k_ref[:, cols].astype(F32), kg_ref[...])
        kt_ref[0, h] = kr.T.astype(BF16)


def _qkprep(pos, proj, qg_tab, kg_tab, seg_mat, bsz, seq, tm):
    t = proj.shape[0]
    d = N_HEADS * LANES
    tiles_per_seq = seq // tm
    qcol = 2
    return pl.pallas_call(
        _qkprep_kernel,
        out_shape=(jax.ShapeDtypeStruct((t, d), BF16),
                   jax.ShapeDtypeStruct((bsz, N_HEADS, LANES, seq), BF16)),
        grid=(t // tm,),
        in_specs=[pl.BlockSpec((tm, 1), lambda i: (i, 0)),
                  pl.BlockSpec((tm, d), lambda i: (i, qcol)),
                  pl.BlockSpec((tm, d), lambda i: (i, qcol + 1)),
                  pl.BlockSpec((1, LANES), lambda i: (0, 0)),
                  pl.BlockSpec((1, LANES), lambda i: (0, 0)),
                  pl.BlockSpec((LANES, LANES), lambda i: (0, 0))],
        out_specs=(pl.BlockSpec((tm, d), lambda i: (i, 0)),
                   pl.BlockSpec((1, N_HEADS, LANES, tm),
                                lambda i: (i // tiles_per_seq, 0, 0, i % tiles_per_seq))),
        compiler_params=_params(("parallel",)),
        name="qkprep",
    )(pos, proj, proj, qg_tab, kg_tab, seg_mat)


def _attn_kernel(lam_ref, q_ref, kt_ref, v_ref, sg_ref, o_ref, q_scr, m_scr, acc_scr, *, tq):
    qi = pl.program_id(2)
    q = q_ref[...]
    q_scr[0] = q[:, :HEAD_DIM]
    q_scr[1] = q[:, HEAD_DIM:]
    m_scr[...] = jnp.full(m_scr.shape, NEG_BIG, F32)
    acc_scr[...] = jnp.zeros(acc_scr.shape, F32)
    ones = jnp.ones((tq, LANES), BF16)

    def step(j, masked):
        off = pl.multiple_of(j * tq, tq)
        vext = jnp.concatenate([v_ref[0, pl.ds(off, tq), :], ones], axis=1)
        s0 = jnp.dot(q_scr[0], kt_ref[0, 0, 0:HEAD_DIM, pl.ds(off, tq)], preferred_element_type=F32)
        s1 = jnp.dot(q_scr[1], kt_ref[0, 0, HEAD_DIM:LANES, pl.ds(off, tq)], preferred_element_type=F32)
        s = jnp.concatenate([s0, s1], axis=0)
        if masked:
            row = lax.broadcasted_iota(jnp.int32, (2 * tq, tq), 0) % tq
            col = lax.broadcasted_iota(jnp.int32, (2 * tq, tq), 1)
            s = jnp.where(col <= row, s, NEG_BIG)
        m_prev = m_scr[...]
        m_next = jnp.maximum(m_prev, jnp.max(s, axis=1, keepdims=True))
        alpha = jnp.exp2(m_prev - m_next)
        p = jnp.exp2(s - m_next).astype(BF16)
        pv = jnp.dot(p, vext, preferred_element_type=F32)
        acc_scr[...] = acc_scr[...] * alpha + pv
        m_scr[...] = m_next

    def body(j, carry):
        step(j, False)
        return carry

    lax.fori_loop(0, qi, body, 0)
    step(qi, True)

    acc = acc_scr[...]
    o = acc[:, :V_DIM] / acc[:, V_DIM:]
    o = o[:tq] - lam_ref[0, 0] * o[tq:]
    ms = jnp.mean(o * o, axis=-1, keepdims=True)
    o_ref[...] = (o * lax.rsqrt(ms + EPS) * (sg_ref[...] * (1.0 - LAMBDA_INIT))).astype(BF16)


def _attention(lam, qn, kt, proj3, subln_g, bsz, seq, tq):
    nq = seq // tq
    vcol = 4 * N_HEADS
    return pl.pallas_call(
        functools.partial(_attn_kernel, tq=tq),
        out_shape=jax.ShapeDtypeStruct((bsz * seq, N_HEADS * V_DIM), BF16),
        grid=(bsz, N_HEADS, nq),
        in_specs=[pl.BlockSpec(memory_space=pltpu.SMEM),
                  pl.BlockSpec((tq, LANES), lambda b, h, i: (b * nq + i, h)),
                  pl.BlockSpec((1, 1, LANES, seq), lambda b, h, i: (b, h, 0, 0)),
                  pl.BlockSpec((1, seq, LANES), lambda b, h, i: (b, 0, vcol + h)),
                  pl.BlockSpec((1, V_DIM), lambda b, h, i: (0, 0))],
        out_specs=pl.BlockSpec((tq, V_DIM), lambda b, h, i: (b * nq + i, h)),
        scratch_shapes=[pltpu.VMEM((2, tq, HEAD_DIM), BF16),
                        pltpu.VMEM((2 * tq, 1), F32),
                        pltpu.VMEM((2 * tq, 2 * V_DIM), F32)],
        compiler_params=_params(("parallel", "parallel", "arbitrary")),
        name="attn",
    )(lam, qn, kt, proj3, subln_g.reshape(1, V_DIM))


def _conv_kernel(ca_ref, cb_ref, hca_ref, hcb_ref, dw_ref, dwb_ref, lng_ref, lnb_ref, w_ref, b_ref,
                 o_ref, u_scr, c_scr, *, ts, tiles_per_seq):
    first = (pl.program_id(0) % tiles_per_seq) == 0
    u_scr[CONV_HALO:, :] = ca_ref[...].astype(F32) * _sigmoid(cb_ref[...].astype(F32))
    halo = hca_ref[...].astype(F32) * _sigmoid(hcb_ref[...].astype(F32))
    u_scr[:CONV_HALO, :] = jnp.where(first, 0.0, halo)

    def col_chunk(c, carry):
        cols = pl.ds(pl.multiple_of(c * LANES, LANES), LANES)
        acc = jnp.zeros((ts, LANES), F32)
        for t in range(CONV_TAPS):
            start = CONV_HALO - (CONV_TAPS - 1) + t
            acc = acc + u_scr[start:start + ts, cols] * dw_ref[t:t + 1, cols]
        c_scr[:, cols] = acc + dwb_ref[:, cols]
        return carry

    lax.fori_loop(0, c_scr.shape[1] // LANES, col_chunk, 0)

    u = c_scr[...]
    mu = jnp.mean(u, axis=-1, keepdims=True)
    uc = u - mu
    var = jnp.mean(uc * uc, axis=-1, keepdims=True)
    y = uc * lax.rsqrt(var + EPS) * lng_ref[...] + lnb_ref[...]
    y = _silu(y).astype(BF16)
    o_ref[...] = (jnp.dot(y, w_ref[...], preferred_element_type=F32) + b_ref[...]).astype(BF16)


def _conv_branch(proj, conv_dw, conv_dw_b, ln_g, ln_b, w_pw2_bf, b_pw2, seq, ts):
    t = proj.shape[0]
    c = w_pw2_bf.shape[0]
    d = w_pw2_bf.shape[1]
    tiles_per_seq = seq // ts
    hb = ts // CONV_HALO
    dw = jnp.zeros((CONV_HALO, c), F32).at[:CONV_TAPS].set(conv_dw)
    row = lambda a: a.reshape(1, -1)
    return pl.pallas_call(
        functools.partial(_conv_kernel, ts=ts, tiles_per_seq=tiles_per_seq),
        out_shape=jax.ShapeDtypeStruct((t, d), BF16),
        grid=(t // ts,),
        in_specs=[pl.BlockSpec((ts, c), lambda i: (i, 0)),
                  pl.BlockSpec((ts, c), lambda i: (i, 1)),
                  pl.BlockSpec((CONV_HALO, c), lambda i: (jnp.maximum(i * hb - 1, 0), 0)),
                  pl.BlockSpec((CONV_HALO, c), lambda i: (jnp.maximum(i * hb - 1, 0), 1)),
                  pl.BlockSpec((CONV_HALO, c), lambda i: (0, 0)),
                  pl.BlockSpec((1, c), lambda i: (0, 0)),
                  pl.BlockSpec((1, c), lambda i: (0, 0)),
                  pl.BlockSpec((1, c), lambda i: (0, 0)),
                  pl.BlockSpec((c, d), lambda i: (0, 0)),
                  pl.BlockSpec((1, d), lambda i: (0, 0))],
        out_specs=pl.BlockSpec((ts, d), lambda i: (i, 0)),
        scratch_shapes=[pltpu.VMEM((ts + CONV_HALO, c), F32),
                        pltpu.VMEM((ts, c), F32)],
        compiler_params=_params(("parallel",)),
        name="conv",
    )(proj, proj, proj, proj, dw, row(conv_dw_b), row(ln_g), row(ln_b), w_pw2_bf, row(b_pw2))


def _merge_kernel(x_ref, yc_ref, ya_ref, gc_ref, ga_ref, mod_ref, g2_ref, wo_ref, wr_ref,
                  x1_ref, h2_ref, lg_ref):
    mod = mod_ref[0]
    merged = (_sigmoid(gc_ref[...].astype(F32)) * yc_ref[...].astype(F32)
              + _sigmoid(ga_ref[...].astype(F32)) * ya_ref[...].astype(F32))
    x1 = x_ref[...] + mod[2:3, :] * jnp.dot(merged.astype(BF16), wo_ref[...], preferred_element_type=F32)
    x1_ref[...] = x1
    ms = jnp.mean(x1 * x1, axis=-1, keepdims=True)
    h2 = x1 * lax.rsqrt(ms + EPS) * (g2_ref[...] * (1.0 + mod[4:5, :])) + mod[3:4, :]
    h2_ref[...] = h2.astype(BF16)
    lg_ref[...] = lax.dot_general(wr_ref[...], h2, (((1,), (1,)), ((), ())),
                                  preferred_element_type=F32, precision=lax.Precision.HIGHEST)


def _merge(x2, y_conv, y_attn, proj, mod8, norm2_g, w_out_bf, w_router_t, seq, tm):
    t, d = x2.shape
    tiles_per_seq = seq // tm
    gcol = 5
    tok = lambda i: (i, 0)
    fixed = lambda i: (0, 0)
    return pl.pallas_call(
        _merge_kernel,
        out_shape=(jax.ShapeDtypeStruct((t, d), F32),
                   jax.ShapeDtypeStruct((t, d), BF16),
                   jax.ShapeDtypeStruct((N_EXPERTS, t), F32)),
        grid=(t // tm,),
        in_specs=[pl.BlockSpec((tm, d), tok),
                  pl.BlockSpec((tm, d), tok),
                  pl.BlockSpec((tm, d), tok),
                  pl.BlockSpec((tm, d), lambda i: (i, gcol)),
                  pl.BlockSpec((tm, d), lambda i: (i, gcol + 1)),
                  pl.BlockSpec((1, 8, d), lambda i: (i // tiles_per_seq, 0, 0)),
                  pl.BlockSpec((1, d), fixed),
                  pl.BlockSpec((d, d), fixed),
                  pl.BlockSpec((N_EXPERTS, d), fixed)],
        out_specs=(pl.BlockSpec((tm, d), tok),
                   pl.BlockSpec((tm, d), tok),
                   pl.BlockSpec((N_EXPERTS, tm), lambda i: (0, i))),
        compiler_params=_params(("parallel",)),
        name="merge",
    )(x2, y_conv, y_attn, proj, proj, mod8, norm2_g.reshape(1, d), w_out_bf, w_router_t)


def _route_kernel(lg_ref, bias_ref, g_ref):
    tm = lg_ref.shape[1]
    scores = _sigmoid(lg_ref[...])
    choice = scores + bias_ref[...]
    sub = lax.broadcasted_iota(jnp.int32, (GROUP_SIZE, tm), 0)
    gs = []
    for g in range(N_GROUPS):
        cg = choice[g * GROUP_SIZE:(g + 1) * GROUP_SIZE, :]
        m1 = jnp.max(cg, axis=0, keepdims=True)
        first = jnp.min(jnp.where(cg == m1, sub, GROUP_SIZE), axis=0, keepdims=True)
        m2 = jnp.max(jnp.where(sub == first, -jnp.inf, cg), axis=0, keepdims=True)
        gs.append(m1 + m2)
    masked = []
    for g in range(N_GROUPS):
        rank = jnp.zeros((1, tm), F32)
        for o in range(N_GROUPS):
            if o == g:
                continue
            beats = (gs[o] >= gs[g]) if o < g else (gs[o] > gs[g])
            rank = rank + jnp.where(beats, 1.0, 0.0)
        keep = rank < TOPK_GROUPS
        masked.append(jnp.where(keep, choice[g * GROUP_SIZE:(g + 1) * GROUP_SIZE, :], -jnp.inf))
    masked = jnp.concatenate(masked, axis=0)
    eidx = lax.broadcasted_iota(jnp.int32, (N_EXPERTS, tm), 0)
    rank = jnp.zeros((N_EXPERTS, tm), F32)
    for o in range(N_EXPERTS):
        other = masked[o:o + 1, :]
        tie = jnp.where(eidx > o, 1.0, 0.0)
        rank = rank + jnp.where(other > masked, 1.0, 0.0) + jnp.where(other == masked, tie, 0.0)
    w = jnp.where(rank < TOP_K, scores, 0.0)
    w = w / jnp.sum(w, axis=0, keepdims=True) * ROUTED_SCALE
    wp = jnp.concatenate([w, jnp.zeros((LANES - N_EXPERTS, tm), F32)], axis=0)
    g_ref[...] = wp.T


def _route(logits_t, router_bias, tm):
    t = logits_t.shape[1]
    return pl.pallas_call(
        _route_kernel,
        out_shape=jax.ShapeDtypeStruct((t, LANES), F32),
        grid=(t // tm,),
        in_specs=[pl.BlockSpec((N_EXPERTS, tm), lambda i: (0, i)),
                  pl.BlockSpec((N_EXPERTS, 1), lambda i: (0, 0))],
        out_specs=pl.BlockSpec((tm, LANES), lambda i: (i, 0)),
        compiler_params=_params(("parallel",)),
        name="route",
    )(logits_t, router_bias.reshape(N_EXPERTS, 1))


def _moe_kernel(h2_ref, x1_ref, gates_ref, mod_ref, wgu_ref, wd_ref, wsgu_ref, wsd_ref, o_ref, acc_scr, *, eb):
    e = pl.program_id(1)
    h2 = h2_ref[...]

    @pl.when(e == 0)
    def _():
        gu = jnp.dot(h2, wsgu_ref[...], preferred_element_type=F32)
        sd = wsgu_ref.shape[1] // 2
        a = (_silu(gu[:, :sd]) * gu[:, sd:]).astype(BF16)
        acc_scr[...] = jnp.dot(a, wsd_ref[...], preferred_element_type=F32)

    gates = gates_ref[...]
    lane = lax.broadcasted_iota(jnp.int32, gates.shape, 1)
    acc = acc_scr[...]
    for k in range(eb):
        gate = jnp.sum(jnp.where(lane == e * eb + k, gates, 0.0), axis=1, keepdims=True)
        gu = jnp.dot(h2, wgu_ref[k], preferred_element_type=F32)
        a = (_silu(gu[:, :EXPERT_DIM]) * gu[:, EXPERT_DIM:] * gate).astype(BF16)
        acc = acc + jnp.dot(a, wd_ref[k], preferred_element_type=F32)
    acc_scr[...] = acc

    @pl.when(e == pl.num_programs(1) - 1)
    def _():
        o_ref[...] = x1_ref[...] + mod_ref[0][5:6, :] * acc


def _moe(h2, x1, gates, mod8, wgu_bf, wd_bf, wsgu_bf, wsd_bf, seq, tm, eb):
    t, d = x1.shape
    tiles_per_seq = seq // tm
    tok = lambda i, e: (i, 0)
    fixed = lambda i, e: (0, 0)
    return pl.pallas_call(
        functools.partial(_moe_kernel, eb=eb),
        out_shape=jax.ShapeDtypeStruct((t, d), F32),
        grid=(t // tm, N_EXPERTS // eb),
        in_specs=[pl.BlockSpec((tm, d), tok),
                  pl.BlockSpec((tm, d), tok),
                  pl.BlockSpec((tm, LANES), tok),
                  pl.BlockSpec((1, 8, d), lambda i, e: (i // tiles_per_seq, 0, 0)),
                  pl.BlockSpec((eb, d, 2 * EXPERT_DIM), lambda i, e: (e, 0, 0)),
                  pl.BlockSpec((eb, EXPERT_DIM, d), lambda i, e: (e, 0, 0)),
                  pl.BlockSpec(wsgu_bf.shape, fixed),
                  pl.BlockSpec(wsd_bf.shape, fixed)],
        out_specs=pl.BlockSpec((tm, d), tok),
        scratch_shapes=[pltpu.VMEM((tm, d), F32)],
        compiler_params=_params(("parallel", "arbitrary")),
        name="moe",
    )(h2, x1, gates, mod8, wgu_bf, wd_bf, wsgu_bf, wsd_bf)


def _tiles(seq):
    pick = lambda want: min(want, seq)
    return dict(inproj=pick(1024), qk=pick(512), attn=pick(512), conv=pick(256), merge=pick(512),
                route=pick(512), moe=pick(1024))


def kernel(x, c, positions, w_ada, b_ada, norm1_g, w_in, conv_dw, conv_dw_b, conv_ln_g, conv_ln_b, w_pw2, b_pw2, q_norm_g, k_norm_g, lambda_q1, lambda_k1, lambda_q2, lambda_k2, subln_g, w_out, norm2_g, w_router, router_bias, w_exp_gu, w_exp_down, w_sh_gu, w_sh_down):
    bsz, seq, d = x.shape
    depth = w_ada.shape[0]
    assert depth == 1 and d == N_HEADS * V_DIM
    tl = _tiles(seq)
    t = bsz * seq
    x2 = x.reshape(t, d)
    pos = positions.astype(F32).reshape(t, 1)
    l = 0

    mod = _adaln(c, w_ada[l], b_ada[l])
    mod8 = jnp.pad(mod.reshape(bsz, 6, d), ((0, 0), (0, 2), (0, 0)))

    proj = _inproj(x2, mod8, norm1_g[l], w_in[l].astype(BF16), seq, tl["inproj"], 1024)

    q_scale = HEAD_DIM ** -0.5 * math.log2(math.e)
    qg_tab = jnp.tile(q_norm_g[l] * q_scale, 2).reshape(1, LANES)
    kg_tab = jnp.tile(k_norm_g[l], 2).reshape(1, LANES)
    seg_id = jnp.arange(LANES) // HEAD_DIM
    seg_mat = (seg_id[:, None] == seg_id[None, :]).astype(BF16)
    qn, kt = _qkprep(pos, proj, qg_tab, kg_tab, seg_mat, bsz, seq, tl["qk"])

    lam = (jnp.exp(jnp.sum(lambda_q1[l] * lambda_k1[l])) - jnp.exp(jnp.sum(lambda_q2[l] * lambda_k2[l]))
           + LAMBDA_INIT).reshape(1, 1)
    y_attn = _attention(lam, qn, kt, proj.reshape(bsz, seq, -1), subln_g[l], bsz, seq, tl["attn"])

    y_conv = _conv_branch(proj, conv_dw[l], conv_dw_b[l], conv_ln_g[l], conv_ln_b[l],
                          w_pw2[l].astype(BF16), b_pw2[l], seq, tl["conv"])

    x1, h2, logits_t = _merge(x2, y_conv, y_attn, proj, mod8, norm2_g[l], w_out[l].astype(BF16),
                              w_router[l].T, seq, tl["merge"])
    gates = _route(logits_t, router_bias[l], tl["route"])
    out = _moe(h2, x1, gates, mod8, w_exp_gu[l].astype(BF16), w_exp_down[l].astype(BF16),
               w_sh_gu[l].astype(BF16), w_sh_down[l].astype(BF16), seq, tl["moe"], 4)
    return out.reshape(bsz, seq, d)
```

```python
import functools
import math

import jax
import jax.numpy as jnp
from jax import lax
from jax.experimental import pallas as pl
from jax.experimental.pallas import tpu as pltpu

F32 = jnp.float32
BF16 = jnp.bfloat16

EPS = 1e-6
CONV_TAPS = 31
CONV_HALO = 32
N_HEADS = 8
HEAD_DIM = 64
V_DIM = 2 * HEAD_DIM
ROPE_THETA = 500000.0
ROPE_DIM = HEAD_DIM // 4
N_EXPERTS = 64
N_GROUPS = 8
GROUP_SIZE = N_EXPERTS // N_GROUPS
TOPK_GROUPS = 4
TOP_K = 8
EXPERT_DIM = 256
ROUTED_SCALE = 2.5
LAMBDA_INIT = 0.8 - 0.6 * math.exp(-0.3 * 0)
LANES = 128
NEG_BIG = -1e30
VMEM_LIMIT = 56 * 1024 * 1024
SAFE_EXP2_RANGE = 100.0


def _params(sem):
    return pltpu.CompilerParams(dimension_semantics=sem, vmem_limit_bytes=VMEM_LIMIT)


def _sigmoid(x):
    return 1.0 / (1.0 + jnp.exp(-x))


def _silu(x):
    return x * _sigmoid(x)


def _adaln_kernel(c_ref, w_ref, b_ref, o_ref):
    c = c_ref[...]
    o_ref[...] = jnp.dot(_silu(c), w_ref[...], preferred_element_type=F32,
                         precision=lax.Precision.HIGHEST) + b_ref[...]


def _adaln(c, w_ada, b_ada):
    bsz, d = c.shape
    n = w_ada.shape[1]
    tn = 1536
    return pl.pallas_call(
        _adaln_kernel,
        out_shape=jax.ShapeDtypeStruct((bsz, n), F32),
        grid=(n // tn,),
        in_specs=[pl.BlockSpec((bsz, d), lambda j: (0, 0)),
                  pl.BlockSpec((d, tn), lambda j: (0, j)),
                  pl.BlockSpec((1, tn), lambda j: (0, j))],
        out_specs=pl.BlockSpec((bsz, tn), lambda j: (0, j)),
        compiler_params=_params(("parallel",)),
        name="adaln",
    )(c, w_ada, b_ada.reshape(1, n))


def _inproj_kernel(x_ref, mod_ref, g_ref, w_ref, o_ref, h_scr):
    @pl.when(pl.program_id(1) == 0)
    def _():
        x = x_ref[...]
        ms = jnp.mean(x * x, axis=-1, keepdims=True)
        mod = mod_ref[0]
        gmul = g_ref[...] * (1.0 + mod[1:2, :])
        h_scr[...] = (x * lax.rsqrt(ms + EPS) * gmul + mod[0:1, :]).astype(BF16)

    o_ref[...] = jnp.dot(h_scr[...], w_ref[...], preferred_element_type=F32).astype(BF16)


def _inproj(x2, mod8, norm_g, w_in_bf, seq, tm, tn):
    t, d = x2.shape
    n = w_in_bf.shape[1]
    tiles_per_seq = seq // tm
    return pl.pallas_call(
        _inproj_kernel,
        out_shape=jax.ShapeDtypeStruct((t, n), BF16),
        grid=(t // tm, n // tn),
        in_specs=[pl.BlockSpec((tm, d), lambda i, j: (i, 0)),
                  pl.BlockSpec((1, 8, d), lambda i, j: (i // tiles_per_seq, 0, 0)),
                  pl.BlockSpec((1, d), lambda i, j: (0, 0)),
                  pl.BlockSpec((d, tn), lambda i, j: (0, j))],
        out_specs=pl.BlockSpec((tm, tn), lambda i, j: (i, j)),
        scratch_shapes=[pltpu.VMEM((tm, d), BF16)],
        compiler_params=_params(("parallel", "arbitrary")),
        name="inproj",
    )(x2, mod8, norm_g.reshape(1, d), w_in_bf)


def _qkvprep_kernel(pos_ref, q_ref, k_ref, v_ref, qg_ref, kg_ref, seg_ref, qt_ref, ko_ref, vt_ref):
    pos = pos_ref[...]
    lane = lax.broadcasted_iota(jnp.int32, (1, LANES), 1)
    in_seg = lane % HEAD_DIM
    half = ROPE_DIM // 2
    freq_idx = (in_seg % half).astype(F32)
    inv_freq = jnp.exp(freq_idx * (-2.0 * math.log(ROPE_THETA) / ROPE_DIM))
    ang = pos * inv_freq
    cos = jnp.cos(ang)
    sin = jnp.sin(ang)
    is_lo = in_seg < half
    is_hi = jnp.logical_and(in_seg >= half, in_seg < ROPE_DIM)
    c_tab = jnp.where(in_seg < ROPE_DIM, cos, 1.0)
    s_lo = jnp.where(is_lo, -sin, 0.0)
    s_hi = jnp.where(is_hi, sin, 0.0)
    seg = seg_ref[...]
    first_half = lane < HEAD_DIM

    def prep(x, g):
        sq = x * x
        hi = sq.astype(BF16)
        lo = (sq - hi.astype(F32)).astype(BF16)
        ssq = (jnp.dot(hi, seg, preferred_element_type=F32)
               + jnp.dot(lo, seg, preferred_element_type=F32))
        xn = x * lax.rsqrt(ssq * (1.0 / HEAD_DIM) + EPS) * g
        up = pltpu.roll(xn, LANES - half, axis=1)
        dn = pltpu.roll(xn, half, axis=1)
        return xn * c_tab + up * s_lo + dn * s_hi

    for h in range(N_HEADS):
        cols = slice(h * LANES, (h + 1) * LANES)
        qr = prep(q_ref[:, cols].astype(F32), qg_ref[...])
        qt_ref[0, h, 0] = jnp.where(first_half, qr, 0.0).T.astype(BF16)
        qt_ref[0, h, 1] = jnp.where(first_half, 0.0, qr).T.astype(BF16)
        ko_ref[:, cols] = prep(k_ref[:, cols].astype(F32), kg_ref[...]).astype(BF16)
        vt_ref[0, h] = v_ref[:, cols].astype(F32).T.astype(BF16)


def _qkvprep(pos, proj, qg_tab, kg_tab, seg_mat, bsz, seq, tm):
    t = proj.shape[0]
    d = N_HEADS * LANES
    tiles_per_seq = seq // tm
    qcol = 2
    return pl.pallas_call(
        _qkvprep_kernel,
        out_shape=(jax.ShapeDtypeStruct((bsz, N_HEADS, 2, LANES, seq), BF16),
                   jax.ShapeDtypeStruct((t, d), BF16),
                   jax.ShapeDtypeStruct((bsz, N_HEADS, LANES, seq), BF16)),
        grid=(t // tm,),
        in_specs=[pl.BlockSpec((tm, 1), lambda i: (i, 0)),
                  pl.BlockSpec((tm, d), lambda i: (i, qcol)),
                  pl.BlockSpec((tm, d), lambda i: (i, qcol + 1)),
                  pl.BlockSpec((tm, d), lambda i: (i, qcol + 2)),
                  pl.BlockSpec((1, LANES), lambda i: (0, 0)),
                  pl.BlockSpec((1, LANES), lambda i: (0, 0)),
                  pl.BlockSpec((LANES, LANES), lambda i: (0, 0))],
        out_specs=(pl.BlockSpec((1, N_HEADS, 2, LANES, tm),
                                lambda i: (i // tiles_per_seq, 0, 0, 0, i % tiles_per_seq)),
                   pl.BlockSpec((tm, d), lambda i: (i, 0)),
                   pl.BlockSpec((1, N_HEADS, LANES, tm),
                                lambda i: (i // tiles_per_seq, 0, 0, i % tiles_per_seq))),
        compiler_params=_params(("parallel",)),
        name="qkvprep",
    )(pos, proj, proj, proj, qg_tab, kg_tab, seg_mat)


ONES_ROWS = 16
KV_UNROLL = 4


def _attn_kernel(sc_ref, qt_ref, k_ref, vt_ref, sg_ref, o_ref, acc_scr, m_scr, *, tq, exact):
    qi = pl.program_id(2)
    lam = sc_ref[0, 0]
    shift = sc_ref[0, 1]
    acc_scr[...] = jnp.zeros(acc_scr.shape, F32)
    if exact:
        m_scr[...] = jnp.full(m_scr.shape, NEG_BIG, F32)
    ones = jnp.ones((ONES_ROWS, tq), BF16)

    def step(j, masked):
        off = pl.multiple_of(j * tq, tq)
        kj = k_ref[0, pl.ds(off, tq), :]
        s = jnp.concatenate([jnp.dot(kj, qt_ref[0, 0, 0], preferred_element_type=F32),
                             jnp.dot(kj, qt_ref[0, 0, 1], preferred_element_type=F32)], axis=1)
        if masked:
            key = lax.broadcasted_iota(jnp.int32, (tq, 2 * tq), 0)
            qry = lax.broadcasted_iota(jnp.int32, (tq, 2 * tq), 1) % tq
            s = jnp.where(key <= qry, s, NEG_BIG)
        vx = jnp.concatenate([vt_ref[0, 0, :, pl.ds(off, tq)], ones], axis=0)
        if exact:
            m_prev = m_scr[...]
            m_next = jnp.maximum(m_prev, jnp.max(s, axis=0, keepdims=True))
            p = jnp.exp2(s - m_next).astype(BF16)
            acc_scr[...] = (acc_scr[...] * jnp.exp2(m_prev - m_next)
                            + jnp.dot(vx, p, preferred_element_type=F32))
            m_scr[...] = m_next
        else:
            p = jnp.exp2(s - shift).astype(BF16)
            acc_scr[...] += jnp.dot(vx, p, preferred_element_type=F32)

    def body(g, carry):
        for u in range(KV_UNROLL):
            step(g * KV_UNROLL + u, False)
        return carry

    n_groups = qi // KV_UNROLL
    lax.fori_loop(0, n_groups, body, 0)

    def tail(j, carry):
        step(j, False)
        return carry

    lax.fori_loop(n_groups * KV_UNROLL, qi, tail, 0)
    step(qi, True)

    acc = acc_scr[...]
    o = acc[:V_DIM, :] / acc[V_DIM:V_DIM + 1, :]
    o = o[:, :tq] - lam * o[:, tq:]
    ms = jnp.mean(o * o, axis=0, keepdims=True)
    y = (o * lax.rsqrt(ms + EPS)).T
    o_ref[...] = (y * (sg_ref[...] * (1.0 - LAMBDA_INIT))).astype(BF16)


def _attention(scal, qt, kn3, vt, subln_g, bsz, seq, tq, exact):
    nq = seq // tq
    return pl.pallas_call(
        functools.partial(_attn_kernel, tq=tq, exact=exact),
        out_shape=jax.ShapeDtypeStruct((bsz * seq, N_HEADS * V_DIM), BF16),
        grid=(bsz, N_HEADS, nq),
        in_specs=[pl.BlockSpec(memory_space=pltpu.SMEM),
                  pl.BlockSpec((1, 1, 2, LANES, tq), lambda b, h, i: (b, h, 0, 0, i)),
                  pl.BlockSpec((1, seq, LANES), lambda b, h, i: (b, 0, h)),
                  pl.BlockSpec((1, 1, LANES, seq), lambda b, h, i: (b, h, 0, 0)),
                  pl.BlockSpec((1, V_DIM), lambda b, h, i: (0, 0))],
        out_specs=pl.BlockSpec((tq, V_DIM), lambda b, h, i: (b * nq + i, h)),
        scratch_shapes=[pltpu.VMEM((V_DIM + ONES_ROWS, 2 * tq), F32),
                        pltpu.VMEM((1, 2 * tq), F32)],
        compiler_params=_params(("parallel", "parallel", "arbitrary")),
        name="attn_exact" if exact else "attn",
    )(scal, qt, kn3, vt, subln_g.reshape(1, V_DIM))


def _conv_kernel(ca_ref, cb_ref, hca_ref, hcb_ref, dw_ref, dwb_ref, lng_ref, lnb_ref, w_ref, b_ref,
                 o_ref, u_scr, c_scr, *, ts, tiles_per_seq):
    first = (pl.program_id(0) % tiles_per_seq) == 0
    u_scr[CONV_HALO:, :] = ca_ref[...].astype(F32) * _sigmoid(cb_ref[...].astype(F32))
    halo = hca_ref[...].astype(F32) * _sigmoid(hcb_ref[...].astype(F32))
    u_scr[:CONV_HALO, :] = jnp.where(first, 0.0, halo)

    def col_chunk(c, carry):
        cols = pl.ds(pl.multiple_of(c * LANES, LANES), LANES)
        acc = jnp.zeros((ts, LANES), F32)
        for t in range(CONV_TAPS):
            start = CONV_HALO - (CONV_TAPS - 1) + t
            acc = acc + u_scr[start:start + ts, cols] * dw_ref[t:t + 1, cols]
        c_scr[:, cols] = acc + dwb_ref[:, cols]
        return carry

    lax.fori_loop(0, c_scr.shape[1] // LANES, col_chunk, 0)

    u = c_scr[...]
    mu = jnp.mean(u, axis=-1, keepdims=True)
    uc = u - mu
    var = jnp.mean(uc * uc, axis=-1, keepdims=True)
    y = uc * lax.rsqrt(var + EPS) * lng_ref[...] + lnb_ref[...]
    y = _silu(y).astype(BF16)
    o_ref[...] = (jnp.dot(y, w_ref[...], preferred_element_type=F32) + b_ref[...]).astype(BF16)


def _conv_branch(proj, conv_dw, conv_dw_b, ln_g, ln_b, w_pw2_bf, b_pw2, seq, ts):
    t = proj.shape[0]
    c = w_pw2_bf.shape[0]
    d = w_pw2_bf.shape[1]
    tiles_per_seq = seq // ts
    hb = ts // CONV_HALO
    dw = jnp.zeros((CONV_HALO, c), F32).at[:CONV_TAPS].set(conv_dw)
    row = lambda a: a.reshape(1, -1)
    return pl.pallas_call(
        functools.partial(_conv_kernel, ts=ts, tiles_per_seq=tiles_per_seq),
        out_shape=jax.ShapeDtypeStruct((t, d), BF16),
        grid=(t // ts,),
        in_specs=[pl.BlockSpec((ts, c), lambda i: (i, 0)),
                  pl.BlockSpec((ts, c), lambda i: (i, 1)),
                  pl.BlockSpec((CONV_HALO, c), lambda i: (jnp.maximum(i * hb - 1, 0), 0)),
                  pl.BlockSpec((CONV_HALO, c), lambda i: (jnp.maximum(i * hb - 1, 0), 1)),
                  pl.BlockSpec((CONV_HALO, c), lambda i: (0, 0)),
                  pl.BlockSpec((1, c), lambda i: (0, 0)),
                  pl.BlockSpec((1, c), lambda i: (0, 0)),
                  pl.BlockSpec((1, c), lambda i: (0, 0)),
                  pl.BlockSpec((c, d), lambda i: (0, 0)),
                  pl.BlockSpec((1, d), lambda i: (0, 0))],
        out_specs=pl.BlockSpec((ts, d), lambda i: (i, 0)),
        scratch_shapes=[pltpu.VMEM((ts + CONV_HALO, c), F32),
                        pltpu.VMEM((ts, c), F32)],
        compiler_params=_params(("parallel",)),
        name="conv",
    )(proj, proj, proj, proj, dw, row(conv_dw_b), row(ln_g), row(ln_b), w_pw2_bf, row(b_pw2))


def _merge_kernel(x_ref, yc_ref, ya_ref, gc_ref, ga_ref, mod_ref, g2_ref, wo_ref, wr_ref,
                  x1_ref, h2_ref, lg_ref):
    mod = mod_ref[0]
    merged = (_sigmoid(gc_ref[...].astype(F32)) * yc_ref[...].astype(F32)
              + _sigmoid(ga_ref[...].astype(F32)) * ya_ref[...].astype(F32))
    x1 = x_ref[...] + mod[2:3, :] * jnp.dot(merged.astype(BF16), wo_ref[...], preferred_element_type=F32)
    x1_ref[...] = x1
    ms = jnp.mean(x1 * x1, axis=-1, keepdims=True)
    h2 = x1 * lax.rsqrt(ms + EPS) * (g2_ref[...] * (1.0 + mod[4:5, :])) + mod[3:4, :]
    h2_ref[...] = h2.astype(BF16)
    lg_ref[...] = lax.dot_general(wr_ref[...], h2, (((1,), (1,)), ((), ())),
                                  preferred_element_type=F32, precision=lax.Precision.HIGHEST)


def _merge(x2, y_conv, y_attn, proj, mod8, norm2_g, w_out_bf, w_router_t, seq, tm):
    t, d = x2.shape
    tiles_per_seq = seq // tm
    gcol = 5
    tok = lambda i: (i, 0)
    fixed = lambda i: (0, 0)
    return pl.pallas_call(
        _merge_kernel,
        out_shape=(jax.ShapeDtypeStruct((t, d), F32),
                   jax.ShapeDtypeStruct((t, d), BF16),
                   jax.ShapeDtypeStruct((N_EXPERTS, t), F32)),
        grid=(t // tm,),
        in_specs=[pl.BlockSpec((tm, d), tok),
                  pl.BlockSpec((tm, d), tok),
                  pl.BlockSpec((tm, d), tok),
                  pl.BlockSpec((tm, d), lambda i: (i, gcol)),
                  pl.BlockSpec((tm, d), lambda i: (i, gcol + 1)),
                  pl.BlockSpec((1, 8, d), lambda i: (i // tiles_per_seq, 0, 0)),
                  pl.BlockSpec((1, d), fixed),
                  pl.BlockSpec((d, d), fixed),
                  pl.BlockSpec((N_EXPERTS, d), fixed)],
        out_specs=(pl.BlockSpec((tm, d), tok),
                   pl.BlockSpec((tm, d), tok),
                   pl.BlockSpec((N_EXPERTS, tm), lambda i: (0, i))),
        compiler_params=_params(("parallel",)),
        name="merge",
    )(x2, y_conv, y_attn, proj, proj, mod8, norm2_g.reshape(1, d), w_out_bf, w_router_t)


def _route_kernel(lg_ref, bias_ref, g_ref):
    tm = lg_ref.shape[1]
    scores = _sigmoid(lg_ref[...])
    choice = scores + bias_ref[...]
    sub = lax.broadcasted_iota(jnp.int32, (GROUP_SIZE, tm), 0)
    gs = []
    for g in range(N_GROUPS):
        cg = choice[g * GROUP_SIZE:(g + 1) * GROUP_SIZE, :]
        m1 = jnp.max(cg, axis=0, keepdims=True)
        first = jnp.min(jnp.where(cg == m1, sub, GROUP_SIZE), axis=0, keepdims=True)
        m2 = jnp.max(jnp.where(sub == first, -jnp.inf, cg), axis=0, keepdims=True)
        gs.append(m1 + m2)
    masked = []
    for g in range(N_GROUPS):
        rank = jnp.zeros((1, tm), F32)
        for o in range(N_GROUPS):
            if o == g:
                continue
            beats = (gs[o] >= gs[g]) if o < g else (gs[o] > gs[g])
            rank = rank + jnp.where(beats, 1.0, 0.0)
        keep = rank < TOPK_GROUPS
        masked.append(jnp.where(keep, choice[g * GROUP_SIZE:(g + 1) * GROUP_SIZE, :], -jnp.inf))
    masked = jnp.concatenate(masked, axis=0)
    eidx = lax.broadcasted_iota(jnp.int32, (N_EXPERTS, tm), 0)
    rank = jnp.zeros((N_EXPERTS, tm), F32)
    for o in range(N_EXPERTS):
        other = masked[o:o + 1, :]
        tie = jnp.where(eidx > o, 1.0, 0.0)
        rank = rank + jnp.where(other > masked, 1.0, 0.0) + jnp.where(other == masked, tie, 0.0)
    w = jnp.where(rank < TOP_K, scores, 0.0)
    w = w / jnp.sum(w, axis=0, keepdims=True) * ROUTED_SCALE
    wp = jnp.concatenate([w, jnp.zeros((LANES - N_EXPERTS, tm), F32)], axis=0)
    g_ref[...] = wp.T


def _route(logits_t, router_bias, tm):
    t = logits_t.shape[1]
    return pl.pallas_call(
        _route_kernel,
        out_shape=jax.ShapeDtypeStruct((t, LANES), F32),
        grid=(t // tm,),
        in_specs=[pl.BlockSpec((N_EXPERTS, tm), lambda i: (0, i)),
                  pl.BlockSpec((N_EXPERTS, 1), lambda i: (0, 0))],
        out_specs=pl.BlockSpec((tm, LANES), lambda i: (i, 0)),
        compiler_params=_params(("parallel",)),
        name="route",
    )(logits_t, router_bias.reshape(N_EXPERTS, 1))


def _moe_kernel(h2_ref, x1_ref, gates_ref, mod_ref, wgu_ref, wd_ref, wsgu_ref, wsd_ref, o_ref, acc_scr, *, eb):
    e = pl.program_id(1)
    h2 = h2_ref[...]

    @pl.when(e == 0)
    def _():
        gu = jnp.dot(h2, wsgu_ref[...], preferred_element_type=F32)
        sd = wsgu_ref.shape[1] // 2
        a = (_silu(gu[:, :sd]) * gu[:, sd:]).astype(BF16)
        acc_scr[...] = jnp.dot(a, wsd_ref[...], preferred_element_type=F32)

    gates = gates_ref[...]
    lane = lax.broadcasted_iota(jnp.int32, gates.shape, 1)
    acc = acc_scr[...]
    for k in range(eb):
        gate = jnp.sum(jnp.where(lane == e * eb + k, gates, 0.0), axis=1, keepdims=True)
        gu = jnp.dot(h2, wgu_ref[k], preferred_element_type=F32)
        a = (_silu(gu[:, :EXPERT_DIM]) * gu[:, EXPERT_DIM:] * gate).astype(BF16)
        acc = acc + jnp.dot(a, wd_ref[k], preferred_element_type=F32)
    acc_scr[...] = acc

    @pl.when(e == pl.num_programs(1) - 1)
    def _():
        o_ref[...] = x1_ref[...] + mod_ref[0][5:6, :] * acc


def _moe(h2, x1, gates, mod8, wgu_bf, wd_bf, wsgu_bf, wsd_bf, seq, tm, eb):
    t, d = x1.shape
    tiles_per_seq = seq // tm
    tok = lambda i, e: (i, 0)
    fixed = lambda i, e: (0, 0)
    return pl.pallas_call(
        functools.partial(_moe_kernel, eb=eb),
        out_shape=jax.ShapeDtypeStruct((t, d), F32),
        grid=(t // tm, N_EXPERTS // eb),
        in_specs=[pl.BlockSpec((tm, d), tok),
                  pl.BlockSpec((tm, d), tok),
                  pl.BlockSpec((tm, LANES), tok),
                  pl.BlockSpec((1, 8, d), lambda i, e: (i // tiles_per_seq, 0, 0)),
                  pl.BlockSpec((eb, d, 2 * EXPERT_DIM), lambda i, e: (e, 0, 0)),
                  pl.BlockSpec((eb, EXPERT_DIM, d), lambda i, e: (e, 0, 0)),
                  pl.BlockSpec(wsgu_bf.shape, fixed),
                  pl.BlockSpec(wsd_bf.shape, fixed)],
        out_specs=pl.BlockSpec((tm, d), tok),
        scratch_shapes=[pltpu.VMEM((tm, d), F32)],
        compiler_params=_params(("parallel", "arbitrary")),
        name="moe",
    )(h2, x1, gates, mod8, wgu_bf, wd_bf, wsgu_bf, wsd_bf)


def _tiles(seq):
    pick = lambda want: min(want, seq)
    return dict(inproj=pick(1024), qk=pick(512), attn=pick(512), conv=pick(256), merge=pick(512),
                route=pick(512), moe=pick(1024))


def kernel(x, c, positions, w_ada, b_ada, norm1_g, w_in, conv_dw, conv_dw_b, conv_ln_g, conv_ln_b, w_pw2, b_pw2, q_norm_g, k_norm_g, lambda_q1, lambda_k1, lambda_q2, lambda_k2, subln_g, w_out, norm2_g, w_router, router_bias, w_exp_gu, w_exp_down, w_sh_gu, w_sh_down):
    bsz, seq, d = x.shape
    depth = w_ada.shape[0]
    assert depth == 1 and d == N_HEADS * V_DIM
    tl = _tiles(seq)
    t = bsz * seq
    x2 = x.reshape(t, d)
    pos = positions.astype(F32).reshape(t, 1)
    l = 0

    mod = _adaln(c, w_ada[l], b_ada[l])
    mod8 = jnp.pad(mod.reshape(bsz, 6, d), ((0, 0), (0, 2), (0, 0)))

    proj = _inproj(x2, mod8, norm1_g[l], w_in[l].astype(BF16), seq, tl["inproj"], 1024)

    q_scale = HEAD_DIM ** -0.5 * math.log2(math.e)
    qg = q_norm_g[l] * q_scale
    qg_tab = jnp.tile(qg, 2).reshape(1, LANES)
    kg_tab = jnp.tile(k_norm_g[l], 2).reshape(1, LANES)
    seg_id = jnp.arange(LANES) // HEAD_DIM
    seg_mat = (seg_id[:, None] == seg_id[None, :]).astype(BF16)
    qt, kn, vt = _qkvprep(pos, proj, qg_tab, kg_tab, seg_mat, bsz, seq, tl["qk"])

    lam = (jnp.exp(jnp.sum(lambda_q1[l] * lambda_k1[l])) - jnp.exp(jnp.sum(lambda_q2[l] * lambda_k2[l]))
           + LAMBDA_INIT)
    bound = 1.02 * HEAD_DIM * jnp.max(jnp.abs(qg)) * jnp.max(jnp.abs(k_norm_g[l]))
    scal = jnp.stack([lam, bound]).astype(F32).reshape(1, 2)
    attn_args = (scal, qt, kn.reshape(bsz, seq, -1), vt, subln_g[l], bsz, seq, tl["attn"])
    y_attn = lax.cond(2.0 * bound < SAFE_EXP2_RANGE,
                      lambda: _attention(*attn_args, exact=False),
                      lambda: _attention(*attn_args, exact=True))

    y_conv = _conv_branch(proj, conv_dw[l], conv_dw_b[l], conv_ln_g[l], conv_ln_b[l],
                          w_pw2[l].astype(BF16), b_pw2[l], seq, tl["conv"])

    x1, h2, logits_t = _merge(x2, y_conv, y_attn, proj, mod8, norm2_g[l], w_out[l].astype(BF16),
                              w_router[l].T, seq, tl["merge"])
    gates = _route(logits_t, router_bias[l], tl["route"])
    out = _moe(h2, x1, gates, mod8, w_exp_gu[l].astype(BF16), w_exp_down[l].astype(BF16),
               w_sh_gu[l].astype(BF16), w_sh_down[l].astype(BF16), seq, tl["moe"], 4)
    return out.reshape(bsz, seq, d)
```

```python
import functools
import math

import jax
import jax.numpy as jnp
from jax import lax
from jax.experimental import pallas as pl
from jax.experimental.pallas import tpu as pltpu

F32 = jnp.float32
BF16 = jnp.bfloat16

EPS = 1e-6
CONV_TAPS = 31
CONV_HALO = 32
N_HEADS = 8
HEAD_DIM = 64
V_DIM = 2 * HEAD_DIM
ROPE_THETA = 500000.0
ROPE_DIM = HEAD_DIM // 4
N_EXPERTS = 64
N_GROUPS = 8
GROUP_SIZE = N_EXPERTS // N_GROUPS
TOPK_GROUPS = 4
TOP_K = 8
EXPERT_DIM = 256
ROUTED_SCALE = 2.5
LAMBDA_INIT = 0.8 - 0.6 * math.exp(-0.3 * 0)
LANES = 128
SUBLANES = 8
NEG_BIG = -1e30
VMEM_LIMIT = 56 * 1024 * 1024
SAFE_EXP2_RANGE = 100.0


def _params(sem):
    return pltpu.CompilerParams(dimension_semantics=sem, vmem_limit_bytes=VMEM_LIMIT)


def _sigmoid(x):
    return 1.0 / (1.0 + jnp.exp(-x))


def _silu(x):
    return x * _sigmoid(x)


def _adaln_kernel(c_ref, w_ref, b_ref, o_ref):
    c = c_ref[...]
    o_ref[...] = jnp.dot(_silu(c), w_ref[...], preferred_element_type=F32,
                         precision=lax.Precision.HIGHEST) + b_ref[...]


def _adaln(c, w_ada, b_ada):
    bsz, d = c.shape
    n = w_ada.shape[1]
    tn = 1536
    return pl.pallas_call(
        _adaln_kernel,
        out_shape=jax.ShapeDtypeStruct((bsz, n), F32),
        grid=(n // tn,),
        in_specs=[pl.BlockSpec((bsz, d), lambda j: (0, 0)),
                  pl.BlockSpec((d, tn), lambda j: (0, j)),
                  pl.BlockSpec((1, tn), lambda j: (0, j))],
        out_specs=pl.BlockSpec((bsz, tn), lambda j: (0, j)),
        compiler_params=_params(("parallel",)),
        name="adaln",
    )(c, w_ada, b_ada.reshape(1, n))


def _inproj_kernel(x_ref, mod_ref, g_ref, w_ref, o_ref, h_scr):
    @pl.when(pl.program_id(1) == 0)
    def _():
        x = x_ref[...]
        ms = jnp.mean(x * x, axis=-1, keepdims=True)
        mod = mod_ref[0]
        gmul = g_ref[...] * (1.0 + mod[1:2, :])
        h_scr[...] = (x * lax.rsqrt(ms + EPS) * gmul + mod[0:1, :]).astype(BF16)

    o_ref[...] = jnp.dot(h_scr[...], w_ref[...], preferred_element_type=F32).astype(BF16)


def _inproj(x2, mod8, norm_g, w_in_bf, seq, tm, tn):
    t, d = x2.shape
    n = w_in_bf.shape[1]
    tiles_per_seq = seq // tm
    return pl.pallas_call(
        _inproj_kernel,
        out_shape=jax.ShapeDtypeStruct((t, n), BF16),
        grid=(t // tm, n // tn),
        in_specs=[pl.BlockSpec((tm, d), lambda i, j: (i, 0)),
                  pl.BlockSpec((1, 8, d), lambda i, j: (i // tiles_per_seq, 0, 0)),
                  pl.BlockSpec((1, d), lambda i, j: (0, 0)),
                  pl.BlockSpec((d, tn), lambda i, j: (0, j))],
        out_specs=pl.BlockSpec((tm, tn), lambda i, j: (i, j)),
        scratch_shapes=[pltpu.VMEM((tm, d), BF16)],
        compiler_params=_params(("parallel", "arbitrary")),
        name="inproj",
    )(x2, mod8, norm_g.reshape(1, d), w_in_bf)


def _qkvprep_kernel(pos_ref, q_ref, k_ref, v_ref, qg_ref, kg_ref, seg_ref, qt_ref, ko_ref, vt_ref):
    pos = pos_ref[...]
    lane = lax.broadcasted_iota(jnp.int32, (1, LANES), 1)
    in_seg = lane % HEAD_DIM
    half = ROPE_DIM // 2
    freq_idx = (in_seg % half).astype(F32)
    inv_freq = jnp.exp(freq_idx * (-2.0 * math.log(ROPE_THETA) / ROPE_DIM))
    ang = pos * inv_freq
    cos = jnp.cos(ang)
    sin = jnp.sin(ang)
    is_lo = in_seg < half
    is_hi = jnp.logical_and(in_seg >= half, in_seg < ROPE_DIM)
    c_tab = jnp.where(in_seg < ROPE_DIM, cos, 1.0)
    s_lo = jnp.where(is_lo, -sin, 0.0)
    s_hi = jnp.where(is_hi, sin, 0.0)
    seg = seg_ref[...]
    first_half = lane < HEAD_DIM

    def prep(x, g):
        sq = x * x
        hi = sq.astype(BF16)
        lo = (sq - hi.astype(F32)).astype(BF16)
        ssq = (jnp.dot(hi, seg, preferred_element_type=F32)
               + jnp.dot(lo, seg, preferred_element_type=F32))
        xn = x * lax.rsqrt(ssq * (1.0 / HEAD_DIM) + EPS) * g
        up = pltpu.roll(xn, LANES - half, axis=1)
        dn = pltpu.roll(xn, half, axis=1)
        return xn * c_tab + up * s_lo + dn * s_hi

    for h in range(N_HEADS):
        cols = slice(h * LANES, (h + 1) * LANES)
        qr = prep(q_ref[:, cols].astype(F32), qg_ref[...])
        qt_ref[0, h, 0] = jnp.where(first_half, qr, 0.0).T.astype(BF16)
        qt_ref[0, h, 1] = jnp.where(first_half, 0.0, qr).T.astype(BF16)
        ko_ref[:, cols] = prep(k_ref[:, cols].astype(F32), kg_ref[...]).astype(BF16)
        vt_ref[0, h] = v_ref[:, cols].astype(F32).T.astype(BF16)


def _qkvprep(pos, proj, qg_tab, kg_tab, seg_mat, bsz, seq, tm):
    t = proj.shape[0]
    d = N_HEADS * LANES
    tiles_per_seq = seq // tm
    qcol = 2
    return pl.pallas_call(
        _qkvprep_kernel,
        out_shape=(jax.ShapeDtypeStruct((bsz, N_HEADS, 2, LANES, seq), BF16),
                   jax.ShapeDtypeStruct((t, d), BF16),
                   jax.ShapeDtypeStruct((bsz, N_HEADS, LANES, seq), BF16)),
        grid=(t // tm,),
        in_specs=[pl.BlockSpec((tm, 1), lambda i: (i, 0)),
                  pl.BlockSpec((tm, d), lambda i: (i, qcol)),
                  pl.BlockSpec((tm, d), lambda i: (i, qcol + 1)),
                  pl.BlockSpec((tm, d), lambda i: (i, qcol + 2)),
                  pl.BlockSpec((1, LANES), lambda i: (0, 0)),
                  pl.BlockSpec((1, LANES), lambda i: (0, 0)),
                  pl.BlockSpec((LANES, LANES), lambda i: (0, 0))],
        out_specs=(pl.BlockSpec((1, N_HEADS, 2, LANES, tm),
                                lambda i: (i // tiles_per_seq, 0, 0, 0, i % tiles_per_seq)),
                   pl.BlockSpec((tm, d), lambda i: (i, 0)),
                   pl.BlockSpec((1, N_HEADS, LANES, tm),
                                lambda i: (i // tiles_per_seq, 0, 0, i % tiles_per_seq))),
        compiler_params=_params(("parallel",)),
        name="qkvprep",
    )(pos, proj, proj, proj, qg_tab, kg_tab, seg_mat)


ONES_ROWS = 16
KV_UNROLL = 4


def _attn_kernel(sc_ref, qt_ref, k_ref, vt_ref, sg_ref, o_ref, acc_scr, m_scr, *, tq, exact):
    qi = pl.program_id(2)
    lam = sc_ref[0, 0]
    shift = sc_ref[0, 1]
    acc_scr[...] = jnp.zeros(acc_scr.shape, F32)
    if exact:
        m_scr[...] = jnp.full(m_scr.shape, NEG_BIG, F32)
    ones = jnp.ones((ONES_ROWS, tq), BF16)

    def step(j, masked):
        off = pl.multiple_of(j * tq, tq)
        kj = k_ref[0, pl.ds(off, tq), :]
        s = jnp.concatenate([jnp.dot(kj, qt_ref[0, 0, 0], preferred_element_type=F32),
                             jnp.dot(kj, qt_ref[0, 0, 1], preferred_element_type=F32)], axis=1)
        if masked:
            key = lax.broadcasted_iota(jnp.int32, (tq, 2 * tq), 0)
            qry = lax.broadcasted_iota(jnp.int32, (tq, 2 * tq), 1) % tq
            s = jnp.where(key <= qry, s, NEG_BIG)
        vx = jnp.concatenate([vt_ref[0, 0, :, pl.ds(off, tq)], ones], axis=0)
        if exact:
            m_prev = m_scr[...]
            m_next = jnp.maximum(m_prev, jnp.max(s, axis=0, keepdims=True))
            p = jnp.exp2(s - m_next).astype(BF16)
            acc_scr[...] = (acc_scr[...] * jnp.exp2(m_prev - m_next)
                            + jnp.dot(vx, p, preferred_element_type=F32))
            m_scr[...] = m_next
        else:
            p = jnp.exp2(s - shift).astype(BF16)
            acc_scr[...] += jnp.dot(vx, p, preferred_element_type=F32)

    def body(g, carry):
        for u in range(KV_UNROLL):
            step(g * KV_UNROLL + u, False)
        return carry

    n_groups = qi // KV_UNROLL
    lax.fori_loop(0, n_groups, body, 0)

    def tail(j, carry):
        step(j, False)
        return carry

    lax.fori_loop(n_groups * KV_UNROLL, qi, tail, 0)
    step(qi, True)

    acc = acc_scr[...]
    o = acc[:V_DIM, :] / acc[V_DIM:V_DIM + 1, :]
    o = o[:, :tq] - lam * o[:, tq:]
    ms = jnp.mean(o * o, axis=0, keepdims=True)
    y = (o * lax.rsqrt(ms + EPS)).T
    o_ref[...] = (y * (sg_ref[...] * (1.0 - LAMBDA_INIT))).astype(BF16)


def _attention(scal, qt, kn3, vt, subln_g, bsz, seq, tq, exact):
    nq = seq // tq
    return pl.pallas_call(
        functools.partial(_attn_kernel, tq=tq, exact=exact),
        out_shape=jax.ShapeDtypeStruct((bsz * seq, N_HEADS * V_DIM), BF16),
        grid=(bsz, N_HEADS, nq),
        in_specs=[pl.BlockSpec(memory_space=pltpu.SMEM),
                  pl.BlockSpec((1, 1, 2, LANES, tq), lambda b, h, i: (b, h, 0, 0, i)),
                  pl.BlockSpec((1, seq, LANES), lambda b, h, i: (b, 0, h)),
                  pl.BlockSpec((1, 1, LANES, seq), lambda b, h, i: (b, h, 0, 0)),
                  pl.BlockSpec((1, V_DIM), lambda b, h, i: (0, 0))],
        out_specs=pl.BlockSpec((tq, V_DIM), lambda b, h, i: (b * nq + i, h)),
        scratch_shapes=[pltpu.VMEM((V_DIM + ONES_ROWS, 2 * tq), F32),
                        pltpu.VMEM((1, 2 * tq), F32)],
        compiler_params=_params(("parallel", "parallel", "arbitrary")),
        name="attn_exact" if exact else "attn",
    )(scal, qt, kn3, vt, subln_g.reshape(1, V_DIM))


def _conv_kernel(ca_ref, cb_ref, hca_ref, hcb_ref, dw_ref, dwb_ref, lng_ref, lnb_ref, w_ref, b_ref,
                 o_ref, u_scr, c_scr, sh_scr, *, ts, tiles_per_seq):
    first = (pl.program_id(0) % tiles_per_seq) == 0
    u_scr[CONV_HALO:, :] = ca_ref[...].astype(F32) * _sigmoid(cb_ref[...].astype(F32))
    halo = hca_ref[...].astype(F32) * _sigmoid(hcb_ref[...].astype(F32))
    u_scr[:CONV_HALO, :] = jnp.where(first, 0.0, halo)

    first_off = CONV_HALO - (CONV_TAPS - 1)
    last_off = first_off + CONV_TAPS - 1

    def col_chunk(c, carry):
        cols = pl.ds(pl.multiple_of(c * LANES, LANES), LANES)
        acc = jnp.zeros((ts, LANES), F32)
        for b in range(SUBLANES):
            a_hi = (last_off - b) // SUBLANES
            n = a_hi * SUBLANES + ts
            sh_scr[:n, :] = u_scr[b:b + n, cols]
            for a in range(a_hi + 1):
                t = a * SUBLANES + b - first_off
                if t >= 0:
                    acc = acc + sh_scr[a * SUBLANES:a * SUBLANES + ts, :] * dw_ref[t:t + 1, cols]
        c_scr[:, cols] = acc + dwb_ref[:, cols]
        return carry

    lax.fori_loop(0, c_scr.shape[1] // LANES, col_chunk, 0)

    u = c_scr[...]
    mu = jnp.mean(u, axis=-1, keepdims=True)
    uc = u - mu
    var = jnp.mean(uc * uc, axis=-1, keepdims=True)
    y = uc * lax.rsqrt(var + EPS) * lng_ref[...] + lnb_ref[...]
    y = _silu(y).astype(BF16)
    o_ref[...] = (jnp.dot(y, w_ref[...], preferred_element_type=F32) + b_ref[...]).astype(BF16)


def _conv_branch(proj, conv_dw, conv_dw_b, ln_g, ln_b, w_pw2_bf, b_pw2, seq, ts):
    t = proj.shape[0]
    c = w_pw2_bf.shape[0]
    d = w_pw2_bf.shape[1]
    tiles_per_seq = seq // ts
    hb = ts // CONV_HALO
    dw = jnp.zeros((CONV_HALO, c), F32).at[:CONV_TAPS].set(conv_dw)
    row = lambda a: a.reshape(1, -1)
    return pl.pallas_call(
        functools.partial(_conv_kernel, ts=ts, tiles_per_seq=tiles_per_seq),
        out_shape=jax.ShapeDtypeStruct((t, d), BF16),
        grid=(t // ts,),
        in_specs=[pl.BlockSpec((ts, c), lambda i: (i, 0)),
                  pl.BlockSpec((ts, c), lambda i: (i, 1)),
                  pl.BlockSpec((CONV_HALO, c), lambda i: (jnp.maximum(i * hb - 1, 0), 0)),
                  pl.BlockSpec((CONV_HALO, c), lambda i: (jnp.maximum(i * hb - 1, 0), 1)),
                  pl.BlockSpec((CONV_HALO, c), lambda i: (0, 0)),
                  pl.BlockSpec((1, c), lambda i: (0, 0)),
                  pl.BlockSpec((1, c), lambda i: (0, 0)),
                  pl.BlockSpec((1, c), lambda i: (0, 0)),
                  pl.BlockSpec((c, d), lambda i: (0, 0)),
                  pl.BlockSpec((1, d), lambda i: (0, 0))],
        out_specs=pl.BlockSpec((ts, d), lambda i: (i, 0)),
        scratch_shapes=[pltpu.VMEM((ts + CONV_HALO, c), F32),
                        pltpu.VMEM((ts, c), F32),
                        pltpu.VMEM((ts + CONV_HALO, LANES), F32)],
        compiler_params=_params(("parallel",)),
        name="conv",
    )(proj, proj, proj, proj, dw, row(conv_dw_b), row(ln_g), row(ln_b), w_pw2_bf, row(b_pw2))


def _merge_kernel(x_ref, yc_ref, ya_ref, gc_ref, ga_ref, mod_ref, g2_ref, wo_ref, wr_ref, wsgu_ref, wsd_ref,
                  x1_ref, h2_ref, lg_ref, sh_ref):
    mod = mod_ref[0]
    merged = (_sigmoid(gc_ref[...].astype(F32)) * yc_ref[...].astype(F32)
              + _sigmoid(ga_ref[...].astype(F32)) * ya_ref[...].astype(F32))
    x1 = x_ref[...] + mod[2:3, :] * jnp.dot(merged.astype(BF16), wo_ref[...], preferred_element_type=F32)
    x1_ref[...] = x1
    ms = jnp.mean(x1 * x1, axis=-1, keepdims=True)
    h2 = x1 * lax.rsqrt(ms + EPS) * (g2_ref[...] * (1.0 + mod[4:5, :])) + mod[3:4, :]
    h2b = h2.astype(BF16)
    h2_ref[...] = h2b
    lg_ref[...] = lax.dot_general(wr_ref[...], h2, (((1,), (1,)), ((), ())),
                                  preferred_element_type=F32, precision=lax.Precision.HIGHEST)
    gu = jnp.dot(h2b, wsgu_ref[...], preferred_element_type=F32)
    sd = wsgu_ref.shape[1] // 2
    a = (_silu(gu[:, :sd]) * gu[:, sd:]).astype(BF16)
    sh_ref[...] = jnp.dot(a, wsd_ref[...], preferred_element_type=F32)


def _merge(x2, y_conv, y_attn, proj, mod8, norm2_g, w_out_bf, w_router_t, wsgu_bf, wsd_bf, seq, tm):
    t, d = x2.shape
    tiles_per_seq = seq // tm
    gcol = 5
    tok = lambda i: (i, 0)
    fixed = lambda i: (0, 0)
    return pl.pallas_call(
        _merge_kernel,
        out_shape=(jax.ShapeDtypeStruct((t, d), F32),
                   jax.ShapeDtypeStruct((t, d), BF16),
                   jax.ShapeDtypeStruct((N_EXPERTS, t), F32),
                   jax.ShapeDtypeStruct((t, d), F32)),
        grid=(t // tm,),
        in_specs=[pl.BlockSpec((tm, d), tok),
                  pl.BlockSpec((tm, d), tok),
                  pl.BlockSpec((tm, d), tok),
                  pl.BlockSpec((tm, d), lambda i: (i, gcol)),
                  pl.BlockSpec((tm, d), lambda i: (i, gcol + 1)),
                  pl.BlockSpec((1, 8, d), lambda i: (i // tiles_per_seq, 0, 0)),
                  pl.BlockSpec((1, d), fixed),
                  pl.BlockSpec((d, d), fixed),
                  pl.BlockSpec((N_EXPERTS, d), fixed),
                  pl.BlockSpec(wsgu_bf.shape, fixed),
                  pl.BlockSpec(wsd_bf.shape, fixed)],
        out_specs=(pl.BlockSpec((tm, d), tok),
                   pl.BlockSpec((tm, d), tok),
                   pl.BlockSpec((N_EXPERTS, tm), lambda i: (0, i)),
                   pl.BlockSpec((tm, d), tok)),
        compiler_params=_params(("parallel",)),
        name="merge",
    )(x2, y_conv, y_attn, proj, proj, mod8, norm2_g.reshape(1, d), w_out_bf, w_router_t, wsgu_bf, wsd_bf)


def _route_kernel(lg_ref, bias_ref, tri_ref, low_ref, g_ref, eid_ref, pos_ref, gk_ref, cnt_ref):
    tm = lg_ref.shape[1]
    scores = _sigmoid(lg_ref[...])
    choice = scores + bias_ref[...]
    sub = lax.broadcasted_iota(jnp.int32, (GROUP_SIZE, tm), 0)
    gs = []
    for g in range(N_GROUPS):
        cg = choice[g * GROUP_SIZE:(g + 1) * GROUP_SIZE, :]
        m1 = jnp.max(cg, axis=0, keepdims=True)
        first = jnp.min(jnp.where(cg == m1, sub, GROUP_SIZE), axis=0, keepdims=True)
        m2 = jnp.max(jnp.where(sub == first, -jnp.inf, cg), axis=0, keepdims=True)
        gs.append(m1 + m2)
    masked = []
    for g in range(N_GROUPS):
        rank = jnp.zeros((1, tm), F32)
        for o in range(N_GROUPS):
            if o == g:
                continue
            beats = (gs[o] >= gs[g]) if o < g else (gs[o] > gs[g])
            rank = rank + jnp.where(beats, 1.0, 0.0)
        keep = rank < TOPK_GROUPS
        masked.append(jnp.where(keep, choice[g * GROUP_SIZE:(g + 1) * GROUP_SIZE, :], -jnp.inf))
    masked = jnp.concatenate(masked, axis=0)
    eidx = lax.broadcasted_iota(jnp.int32, (N_EXPERTS, tm), 0)
    rank = jnp.zeros((N_EXPERTS, tm), F32)
    for o in range(N_EXPERTS):
        other = masked[o:o + 1, :]
        tie = jnp.where(eidx > o, 1.0, 0.0)
        rank = rank + jnp.where(other > masked, 1.0, 0.0) + jnp.where(other == masked, tie, 0.0)
    chosen = rank < TOP_K
    w = jnp.where(chosen, scores, 0.0)
    w = w / jnp.sum(w, axis=0, keepdims=True) * ROUTED_SCALE
    wp = jnp.concatenate([w, jnp.zeros((LANES - N_EXPERTS, tm), F32)], axis=0)
    g_ref[...] = wp.T
    onehot = jnp.where(chosen, 1.0, 0.0)
    before = jnp.dot(onehot.astype(BF16), tri_ref[...], preferred_element_type=F32)
    count = jnp.sum(onehot, axis=1, keepdims=True)
    padded = jnp.floor((count + (MOE_ROW_ALIGN - 1)) * (1.0 / MOE_ROW_ALIGN)) * MOE_ROW_ALIGN
    start = jnp.dot(low_ref[...], jnp.broadcast_to(padded, (N_EXPERTS, LANES)).astype(BF16),
                    preferred_element_type=F32)[:, :1]
    pos = start + before
    eid = eidx.astype(F32)
    for k in range(TOP_K):
        mine = jnp.logical_and(chosen, rank == k)
        eid_ref[k:k + 1, :] = jnp.sum(jnp.where(mine, eid, 0.0), axis=0, keepdims=True)
        pos_ref[k:k + 1, :] = jnp.sum(jnp.where(mine, pos, 0.0), axis=0, keepdims=True)
        gk_ref[k:k + 1, :] = jnp.sum(jnp.where(mine, w, 0.0), axis=0, keepdims=True)
    cnt_ref[...] = jnp.broadcast_to(count, cnt_ref.shape)


def _route(logits_t, router_bias, tm):
    t = logits_t.shape[1]
    n = t // tm
    idx = jnp.arange(tm)
    tri = (idx[:, None] < idx[None, :]).astype(BF16)
    e = jnp.arange(N_EXPERTS)
    low = jnp.logical_and(e[:, None] // MOE_EXPERT_BLOCK == e[None, :] // MOE_EXPERT_BLOCK,
                          e[None, :] < e[:, None]).astype(BF16)
    tok = lambda i: (0, i)
    return pl.pallas_call(
        _route_kernel,
        out_shape=(jax.ShapeDtypeStruct((t, LANES), F32),
                   jax.ShapeDtypeStruct((TOP_K, t), F32),
                   jax.ShapeDtypeStruct((TOP_K, t), F32),
                   jax.ShapeDtypeStruct((TOP_K, t), F32),
                   jax.ShapeDtypeStruct((n * N_EXPERTS, LANES), F32)),
        grid=(n,),
        in_specs=[pl.BlockSpec((N_EXPERTS, tm), tok),
                  pl.BlockSpec((N_EXPERTS, 1), lambda i: (0, 0)),
                  pl.BlockSpec((tm, tm), lambda i: (0, 0)),
                  pl.BlockSpec((N_EXPERTS, N_EXPERTS), lambda i: (0, 0))],
        out_specs=(pl.BlockSpec((tm, LANES), lambda i: (i, 0)),
                   pl.BlockSpec((TOP_K, tm), tok),
                   pl.BlockSpec((TOP_K, tm), tok),
                   pl.BlockSpec((TOP_K, tm), tok),
                   pl.BlockSpec((N_EXPERTS, LANES), lambda i: (i, 0))),
        compiler_params=_params(("parallel",)),
        name="route",
    )(logits_t, router_bias.reshape(N_EXPERTS, 1), tri, low)


def _moe_kernel(h2_ref, x1_ref, sh_ref, gates_ref, mod_ref, wgu_ref, wd_ref, o_ref, acc_scr, *, eb):
    e = pl.program_id(1)
    h2 = h2_ref[...]

    @pl.when(e == 0)
    def _():
        acc_scr[...] = sh_ref[...]

    gates = gates_ref[...]
    lane = lax.broadcasted_iota(jnp.int32, gates.shape, 1)
    acc = acc_scr[...]
    for k in range(eb):
        gate = jnp.sum(jnp.where(lane == e * eb + k, gates, 0.0), axis=1, keepdims=True)
        gu = jnp.dot(h2, wgu_ref[k], preferred_element_type=F32)
        a = (_silu(gu[:, :EXPERT_DIM]) * gu[:, EXPERT_DIM:] * gate).astype(BF16)
        acc = acc + jnp.dot(a, wd_ref[k], preferred_element_type=F32)
    acc_scr[...] = acc

    @pl.when(e == pl.num_programs(1) - 1)
    def _():
        o_ref[...] = x1_ref[...] + mod_ref[0][5:6, :] * acc


def _moe(h2, x1, shared, gates, mod8, wgu_bf, wd_bf, seq, tm, eb):
    t, d = x1.shape
    tiles_per_seq = seq // tm
    tok = lambda i, e: (i, 0)
    return pl.pallas_call(
        functools.partial(_moe_kernel, eb=eb),
        out_shape=jax.ShapeDtypeStruct((t, d), F32),
        grid=(t // tm, N_EXPERTS // eb),
        in_specs=[pl.BlockSpec((tm, d), tok),
                  pl.BlockSpec((tm, d), tok),
                  pl.BlockSpec((tm, d), tok),
                  pl.BlockSpec((tm, LANES), tok),
                  pl.BlockSpec((1, 8, d), lambda i, e: (i // tiles_per_seq, 0, 0)),
                  pl.BlockSpec((eb, d, 2 * EXPERT_DIM), lambda i, e: (e, 0, 0)),
                  pl.BlockSpec((eb, EXPERT_DIM, d), lambda i, e: (e, 0, 0))],
        out_specs=pl.BlockSpec((tm, d), tok),
        scratch_shapes=[pltpu.VMEM((tm, d), F32)],
        compiler_params=_params(("parallel", "arbitrary")),
        name="moe_dense",
    )(h2, x1, shared, gates, mod8, wgu_bf, wd_bf)


MOE_WINDOW = 256
MOE_WINDOWS_PER_STEP = 4
MOE_EXPERT_BLOCK = 8
MOE_ROW_ALIGN = 16
MOE_BLOCK_ROWS = 768
MOE_RUN_ROWS = 64
MOE_RUN_PASSES = 3
MOE_PICK_CHUNK = 256


def _moe_sparse_kernel(start_ref, count_ref, blk_ref, acc_ref, h2_ref, eid_ref, pos_ref, gk_ref, x1_ref, mod_ref,
                       wgu_ref, wd_ref, o_ref, xs_scr, ys_scr, q_scr, *, eb, n_blocks, win, nwin, rows):
    blk = blk_ref[0]
    step = pl.program_id(0)
    run = MOE_RUN_ROWS
    @pl.when(step == 0)
    def _():
        ys_scr[...] = jnp.zeros(ys_scr.shape, BF16)
        xs_scr[:, rows:, :] = jnp.zeros((nwin, xs_scr.shape[1] - rows, xs_scr.shape[2]), BF16)
    for j in range(nwin):
        tok = slice(j * win, (j + 1) * win)
        mine = jnp.floor(eid_ref[:, tok] * (1.0 / eb)) == blk.astype(F32)
        pos = jnp.where(mine, pos_ref[:, tok], -1.0)
        gate = gk_ref[:, tok].astype(BF16)
        for c0 in range(0, rows, MOE_PICK_CHUNK):
            n = min(MOE_PICK_CHUNK, rows - c0)
            row = lax.broadcasted_iota(jnp.int32, (n, win), 0).astype(F32).astype(BF16)
            loc = pos - c0
            loc = jnp.where(jnp.logical_and(loc >= 0, loc < n), loc, -1.0).astype(BF16)
            zero = jnp.zeros((n, win), BF16)
            back = zero
            for k in range(TOP_K):
                back = jnp.where(row == loc[k:k + 1, :], gate[k:k + 1, :], back)
            pick = jnp.where(back != zero, jnp.ones((n, win), BF16), zero)
            q_scr[j, c0:c0 + n, :] = back
            xs_scr[j, c0:c0 + n, :] = jnp.dot(pick, h2_ref[tok, :],
                                              preferred_element_type=F32).astype(BF16)

    def ffn(k, x):
        gu = jnp.dot(x, wgu_ref[k], preferred_element_type=F32)
        a = (_silu(gu[:, :EXPERT_DIM]) * gu[:, EXPERT_DIM:]).astype(BF16)
        return jnp.dot(a, wd_ref[k], preferred_element_type=F32)

    starts, counts = [], []
    for k in range(eb):
        e = blk * eb + k
        starts.append([pl.multiple_of(start_ref[step * nwin + j, e], MOE_ROW_ALIGN) for j in range(nwin)])
        counts.append([count_ref[step * nwin + j, e] for j in range(nwin)])
        y = ffn(k, jnp.concatenate([xs_scr[j, pl.ds(starts[k][j], run), :] for j in range(nwin)], axis=0))
        for j in range(nwin):
            ys_scr[j, pl.ds(starts[k][j], run), :] = y[j * run:(j + 1) * run].astype(BF16)
    for p in range(1, MOE_RUN_PASSES):
        for k in range(eb):
            longest = functools.reduce(jnp.maximum, counts[k])

            @pl.when(longest > p * run)
            def _(k=k, p=p):
                more = [pl.multiple_of(starts[k][j] + p * run, MOE_ROW_ALIGN) for j in range(nwin)]
                y = ffn(k, jnp.concatenate([xs_scr[j, pl.ds(more[j], run), :] for j in range(nwin)], axis=0))
                idx = lax.broadcasted_iota(jnp.int32, (run, y.shape[1]), 0) + p * run
                for j in range(nwin):
                    old = ys_scr[j, pl.ds(more[j], run), :].astype(F32)
                    keep = jnp.where(idx < counts[k][j], y[j * run:(j + 1) * run], old)
                    ys_scr[j, pl.ds(more[j], run), :] = keep.astype(BF16)
    is_last = blk == n_blocks - 1
    gmul = mod_ref[0][5:6, :]
    for j in range(nwin):
        tok = slice(j * win, (j + 1) * win)
        total = acc_ref[tok, :] + lax.dot_general(q_scr[j], ys_scr[j, :rows, :], (((0,), (0,)), ((), ())),
                                                  preferred_element_type=F32)
        o_ref[tok, :] = jnp.where(is_last, x1_ref[tok, :] + gmul * total, total)


def _moe_sparse(h2, x1, shared, eid_k, pos_k, gate_k, run_start, run_count, mod8, wgu_bf, wd_bf, seq, win, nwin, eb):
    t, d = x1.shape
    n_eb = N_EXPERTS // eb
    w = win * nwin
    steps_per_seq = seq // w
    rows = MOE_BLOCK_ROWS
    tokw = lambda i, *_: (i, 0)
    lists = lambda i, *_: (0, i)
    once = dict(pipeline_mode=pl.Buffered(1))
    grid_spec = pltpu.PrefetchScalarGridSpec(
        num_scalar_prefetch=3,
        grid=(t // w,),
        in_specs=[pl.BlockSpec((w, d), tokw),
                  pl.BlockSpec((w, d), tokw),
                  pl.BlockSpec((TOP_K, w), lists),
                  pl.BlockSpec((TOP_K, w), lists),
                  pl.BlockSpec((TOP_K, w), lists),
                  pl.BlockSpec((w, d), lambda i, s, c, b: (jnp.where(b[0] == n_eb - 1, i, 0), 0), **once),
                  pl.BlockSpec((1, 8, d), lambda i, *_: (i // steps_per_seq, 0, 0)),
                  pl.BlockSpec((eb, d, 2 * EXPERT_DIM), lambda i, s, c, b: (b[0], 0, 0), **once),
                  pl.BlockSpec((eb, EXPERT_DIM, d), lambda i, s, c, b: (b[0], 0, 0), **once)],
        out_specs=pl.BlockSpec((w, d), tokw),
        scratch_shapes=[pltpu.VMEM((nwin, rows + MOE_RUN_PASSES * MOE_RUN_ROWS, d), BF16),
                        pltpu.VMEM((nwin, rows + MOE_RUN_PASSES * MOE_RUN_ROWS, d), BF16),
                        pltpu.VMEM((nwin, rows, win), BF16)])
    call = pl.pallas_call(
        functools.partial(_moe_sparse_kernel, eb=eb, n_blocks=n_eb, win=win, nwin=nwin, rows=rows),
        out_shape=jax.ShapeDtypeStruct((t, d), F32),
        grid_spec=grid_spec,
        compiler_params=_params(("arbitrary",)),
        name="moe_sparse",
    )
    acc = shared
    for b in range(n_eb):
        acc = call(run_start, run_count, jnp.full((1,), b, jnp.int32), acc, h2, eid_k, pos_k, gate_k, x1, mod8,
                   wgu_bf, wd_bf)
    return acc


def _dispatch_tables(count):
    n_win = count.shape[0]
    padded = -(-count // MOE_ROW_ALIGN) * MOE_ROW_ALIGN
    blocks = padded.reshape(n_win, N_EXPERTS // MOE_EXPERT_BLOCK, MOE_EXPERT_BLOCK)
    start = (jnp.cumsum(blocks, axis=-1) - blocks).reshape(n_win, N_EXPERTS)
    fits = jnp.logical_and(jnp.max(count) <= MOE_RUN_PASSES * MOE_RUN_ROWS, jnp.max(jnp.sum(blocks, axis=-1)) <= MOE_BLOCK_ROWS)
    return start.astype(jnp.int32), fits


def _tiles(seq):
    pick = lambda want: min(want, seq)
    return dict(inproj=pick(1024), qk=pick(512), attn=pick(512), conv=pick(256), merge=pick(512),
                route=pick(MOE_WINDOW), moe=pick(1024))


def kernel(x, c, positions, w_ada, b_ada, norm1_g, w_in, conv_dw, conv_dw_b, conv_ln_g, conv_ln_b, w_pw2, b_pw2, q_norm_g, k_norm_g, lambda_q1, lambda_k1, lambda_q2, lambda_k2, subln_g, w_out, norm2_g, w_router, router_bias, w_exp_gu, w_exp_down, w_sh_gu, w_sh_down):
    bsz, seq, d = x.shape
    depth = w_ada.shape[0]
    assert depth == 1 and d == N_HEADS * V_DIM
    tl = _tiles(seq)
    t = bsz * seq
    x2 = x.reshape(t, d)
    pos = positions.astype(F32).reshape(t, 1)
    l = 0

    mod = _adaln(c, w_ada[l], b_ada[l])
    mod8 = jnp.pad(mod.reshape(bsz, 6, d), ((0, 0), (0, 2), (0, 0)))

    proj = _inproj(x2, mod8, norm1_g[l], w_in[l].astype(BF16), seq, tl["inproj"], 1024)

    q_scale = HEAD_DIM ** -0.5 * math.log2(math.e)
    qg = q_norm_g[l] * q_scale
    qg_tab = jnp.tile(qg, 2).reshape(1, LANES)
    kg_tab = jnp.tile(k_norm_g[l], 2).reshape(1, LANES)
    seg_id = jnp.arange(LANES) // HEAD_DIM
    seg_mat = (seg_id[:, None] == seg_id[None, :]).astype(BF16)
    qt, kn, vt = _qkvprep(pos, proj, qg_tab, kg_tab, seg_mat, bsz, seq, tl["qk"])

    lam = (jnp.exp(jnp.sum(lambda_q1[l] * lambda_k1[l])) - jnp.exp(jnp.sum(lambda_q2[l] * lambda_k2[l]))
           + LAMBDA_INIT)
    bound = 1.02 * HEAD_DIM * jnp.max(jnp.abs(qg)) * jnp.max(jnp.abs(k_norm_g[l]))
    scal = jnp.stack([lam, bound]).astype(F32).reshape(1, 2)
    attn_args = (scal, qt, kn.reshape(bsz, seq, -1), vt, subln_g[l], bsz, seq, tl["attn"])
    y_attn = lax.cond(2.0 * bound < SAFE_EXP2_RANGE,
                      lambda: _attention(*attn_args, exact=False),
                      lambda: _attention(*attn_args, exact=True))

    y_conv = _conv_branch(proj, conv_dw[l], conv_dw_b[l], conv_ln_g[l], conv_ln_b[l],
                          w_pw2[l].astype(BF16), b_pw2[l], seq, tl["conv"])

    x1, h2, logits_t, shared = _merge(x2, y_conv, y_attn, proj, mod8, norm2_g[l], w_out[l].astype(BF16),
                                      w_router[l].T, w_sh_gu[l].astype(BF16), w_sh_down[l].astype(BF16),
                                      seq, tl["merge"])
    gates, eid_k, pos_k, gate_k, count = _route(logits_t, router_bias[l], tl["route"])
    wgu_bf = w_exp_gu[l].astype(BF16)
    wd_bf = w_exp_down[l].astype(BF16)
    nwin = min(MOE_WINDOWS_PER_STEP, seq // tl["route"])
    count = count[:, 0].reshape(-1, N_EXPERTS).astype(jnp.int32)
    run_start, fits = _dispatch_tables(count)
    out = lax.cond(fits,
                   lambda: _moe_sparse(h2, x1, shared, eid_k, pos_k, gate_k, run_start, count, mod8,
                                       wgu_bf, wd_bf, seq, tl["route"], nwin, MOE_EXPERT_BLOCK),
                   lambda: _moe(h2, x1, shared, gates, mod8, wgu_bf, wd_bf, seq, tl["moe"], 4))
    return out.reshape(bsz, seq, d)
```

```python
import functools
import math

import jax
import jax.numpy as jnp
from jax import lax
from jax.experimental import pallas as pl
from jax.experimental.pallas import tpu as pltpu

F32 = jnp.float32
BF16 = jnp.bfloat16

EPS = 1e-6
CONV_TAPS = 31
CONV_HALO = 32
N_HEADS = 8
HEAD_DIM = 64
V_DIM = 2 * HEAD_DIM
ROPE_THETA = 500000.0
ROPE_DIM = HEAD_DIM // 4
N_EXPERTS = 64
N_GROUPS = 8
GROUP_SIZE = N_EXPERTS // N_GROUPS
TOPK_GROUPS = 4
TOP_K = 8
EXPERT_DIM = 256
ROUTED_SCALE = 2.5
LAMBDA_INIT = 0.8 - 0.6 * math.exp(-0.3 * 0)
LANES = 128
SUBLANES = 8
NEG_BIG = -1e30
VMEM_LIMIT = 56 * 1024 * 1024
SAFE_EXP2_RANGE = 100.0


def _params(sem):
    return pltpu.CompilerParams(dimension_semantics=sem, vmem_limit_bytes=VMEM_LIMIT)


def _sigmoid(x):
    return 1.0 / (1.0 + jnp.exp(-x))


def _silu(x):
    return x * _sigmoid(x)


def _adaln_kernel(c_ref, w_ref, b_ref, o_ref):
    c = c_ref[...]
    o_ref[...] = jnp.dot(_silu(c), w_ref[...], preferred_element_type=F32,
                         precision=lax.Precision.HIGHEST) + b_ref[...]


def _adaln(c, w_ada, b_ada):
    bsz, d = c.shape
    n = w_ada.shape[1]
    tn = 1536
    return pl.pallas_call(
        _adaln_kernel,
        out_shape=jax.ShapeDtypeStruct((bsz, n), F32),
        grid=(n // tn,),
        in_specs=[pl.BlockSpec((bsz, d), lambda j: (0, 0)),
                  pl.BlockSpec((d, tn), lambda j: (0, j)),
                  pl.BlockSpec((1, tn), lambda j: (0, j))],
        out_specs=pl.BlockSpec((bsz, tn), lambda j: (0, j)),
        compiler_params=_params(("parallel",)),
        name="adaln",
    )(c, w_ada, b_ada.reshape(1, n))


def _inproj_kernel(x_ref, mod_ref, g_ref, w_ref, o_ref, h_scr):
    @pl.when(pl.program_id(1) == 0)
    def _():
        x = x_ref[...]
        ms = jnp.mean(x * x, axis=-1, keepdims=True)
        mod = mod_ref[0]
        gmul = g_ref[...] * (1.0 + mod[1:2, :])
        h_scr[...] = (x * lax.rsqrt(ms + EPS) * gmul + mod[0:1, :]).astype(BF16)

    o_ref[...] = jnp.dot(h_scr[...], w_ref[...], preferred_element_type=F32).astype(BF16)


def _inproj(x2, mod8, norm_g, w_in_bf, seq, tm, tn):
    t, d = x2.shape
    n = w_in_bf.shape[1]
    tiles_per_seq = seq // tm
    return pl.pallas_call(
        _inproj_kernel,
        out_shape=jax.ShapeDtypeStruct((t, n), BF16),
        grid=(t // tm, n // tn),
        in_specs=[pl.BlockSpec((tm, d), lambda i, j: (i, 0)),
                  pl.BlockSpec((1, 8, d), lambda i, j: (i // tiles_per_seq, 0, 0)),
                  pl.BlockSpec((1, d), lambda i, j: (0, 0)),
                  pl.BlockSpec((d, tn), lambda i, j: (0, j))],
        out_specs=pl.BlockSpec((tm, tn), lambda i, j: (i, j)),
        scratch_shapes=[pltpu.VMEM((tm, d), BF16)],
        compiler_params=_params(("parallel", "arbitrary")),
        name="inproj",
    )(x2, mod8, norm_g.reshape(1, d), w_in_bf)


def _qkvprep_kernel(pos_ref, q_ref, k_ref, v_ref, qg_ref, kg_ref, seg_ref, qt_ref, ko_ref, vt_ref):
    pos = pos_ref[...]
    lane = lax.broadcasted_iota(jnp.int32, (1, LANES), 1)
    in_seg = lane % HEAD_DIM
    half = ROPE_DIM // 2
    freq_idx = (in_seg % half).astype(F32)
    inv_freq = jnp.exp(freq_idx * (-2.0 * math.log(ROPE_THETA) / ROPE_DIM))
    ang = pos * inv_freq
    cos = jnp.cos(ang)
    sin = jnp.sin(ang)
    is_lo = in_seg < half
    is_hi = jnp.logical_and(in_seg >= half, in_seg < ROPE_DIM)
    c_tab = jnp.where(in_seg < ROPE_DIM, cos, 1.0)
    s_lo = jnp.where(is_lo, -sin, 0.0)
    s_hi = jnp.where(is_hi, sin, 0.0)
    seg = seg_ref[...]
    first_half = lane < HEAD_DIM

    def prep(x, g):
        sq = x * x
        hi = sq.astype(BF16)
        lo = (sq - hi.astype(F32)).astype(BF16)
        ssq = (jnp.dot(hi, seg, preferred_element_type=F32)
               + jnp.dot(lo, seg, preferred_element_type=F32))
        xn = x * lax.rsqrt(ssq * (1.0 / HEAD_DIM) + EPS) * g
        up = pltpu.roll(xn, LANES - half, axis=1)
        dn = pltpu.roll(xn, half, axis=1)
        return xn * c_tab + up * s_lo + dn * s_hi

    for h in range(N_HEADS):
        cols = slice(h * LANES, (h + 1) * LANES)
        qr = prep(q_ref[:, cols].astype(F32), qg_ref[...])
        qt_ref[0, h, 0] = jnp.where(first_half, qr, 0.0).T.astype(BF16)
        qt_ref[0, h, 1] = jnp.where(first_half, 0.0, qr).T.astype(BF16)
        ko_ref[:, cols] = prep(k_ref[:, cols].astype(F32), kg_ref[...]).astype(BF16)
        vt_ref[0, h] = v_ref[:, cols].astype(F32).T.astype(BF16)


def _qkvprep(pos, proj, qg_tab, kg_tab, seg_mat, bsz, seq, tm):
    t = proj.shape[0]
    d = N_HEADS * LANES
    tiles_per_seq = seq // tm
    qcol = 2
    return pl.pallas_call(
        _qkvprep_kernel,
        out_shape=(jax.ShapeDtypeStruct((bsz, N_HEADS, 2, LANES, seq), BF16),
                   jax.ShapeDtypeStruct((t, d), BF16),
                   jax.ShapeDtypeStruct((bsz, N_HEADS, LANES, seq), BF16)),
        grid=(t // tm,),
        in_specs=[pl.BlockSpec((tm, 1), lambda i: (i, 0)),
                  pl.BlockSpec((tm, d), lambda i: (i, qcol)),
                  pl.BlockSpec((tm, d), lambda i: (i, qcol + 1)),
                  pl.BlockSpec((tm, d), lambda i: (i, qcol + 2)),
                  pl.BlockSpec((1, LANES), lambda i: (0, 0)),
                  pl.BlockSpec((1, LANES), lambda i: (0, 0)),
                  pl.BlockSpec((LANES, LANES), lambda i: (0, 0))],
        out_specs=(pl.BlockSpec((1, N_HEADS, 2, LANES, tm),
                                lambda i: (i // tiles_per_seq, 0, 0, 0, i % tiles_per_seq)),
                   pl.BlockSpec((tm, d), lambda i: (i, 0)),
                   pl.BlockSpec((1, N_HEADS, LANES, tm),
                                lambda i: (i // tiles_per_seq, 0, 0, i % tiles_per_seq))),
        compiler_params=_params(("parallel",)),
        name="qkvprep",
    )(pos, proj, proj, proj, qg_tab, kg_tab, seg_mat)


ONES_ROWS = 16
KV_UNROLL = 4


def _attn_kernel(sc_ref, qt_ref, k_ref, vt_ref, sg_ref, o_ref, acc_scr, m_scr, *, tq, exact):
    qi = pl.program_id(2)
    lam = sc_ref[0, 0]
    shift = sc_ref[0, 1]
    acc_scr[...] = jnp.zeros(acc_scr.shape, F32)
    if exact:
        m_scr[...] = jnp.full(m_scr.shape, NEG_BIG, F32)
    ones = jnp.ones((ONES_ROWS, tq), BF16)

    def step(j, masked):
        off = pl.multiple_of(j * tq, tq)
        kj = k_ref[0, pl.ds(off, tq), :]
        s = jnp.concatenate([jnp.dot(kj, qt_ref[0, 0, 0], preferred_element_type=F32),
                             jnp.dot(kj, qt_ref[0, 0, 1], preferred_element_type=F32)], axis=1)
        if masked:
            key = lax.broadcasted_iota(jnp.int32, (tq, 2 * tq), 0)
            qry = lax.broadcasted_iota(jnp.int32, (tq, 2 * tq), 1) % tq
            s = jnp.where(key <= qry, s, NEG_BIG)
        vx = jnp.concatenate([vt_ref[0, 0, :, pl.ds(off, tq)], ones], axis=0)
        if exact:
            m_prev = m_scr[...]
            m_next = jnp.maximum(m_prev, jnp.max(s, axis=0, keepdims=True))
            p = jnp.exp2(s - m_next).astype(BF16)
            acc_scr[...] = (acc_scr[...] * jnp.exp2(m_prev - m_next)
                            + jnp.dot(vx, p, preferred_element_type=F32))
            m_scr[...] = m_next
        else:
            p = jnp.exp2(s - shift).astype(BF16)
            acc_scr[...] += jnp.dot(vx, p, preferred_element_type=F32)

    def body(g, carry):
        for u in range(KV_UNROLL):
            step(g * KV_UNROLL + u, False)
        return carry

    n_groups = qi // KV_UNROLL
    lax.fori_loop(0, n_groups, body, 0)

    def tail(j, carry):
        step(j, False)
        return carry

    lax.fori_loop(n_groups * KV_UNROLL, qi, tail, 0)
    step(qi, True)

    acc = acc_scr[...]
    o = acc[:V_DIM, :] / acc[V_DIM:V_DIM + 1, :]
    o = o[:, :tq] - lam * o[:, tq:]
    ms = jnp.mean(o * o, axis=0, keepdims=True)
    y = (o * lax.rsqrt(ms + EPS)).T
    o_ref[...] = (y * (sg_ref[...] * (1.0 - LAMBDA_INIT))).astype(BF16)


def _attention(scal, qt, kn3, vt, subln_g, bsz, seq, tq, exact):
    nq = seq // tq
    return pl.pallas_call(
        functools.partial(_attn_kernel, tq=tq, exact=exact),
        out_shape=jax.ShapeDtypeStruct((bsz * seq, N_HEADS * V_DIM), BF16),
        grid=(bsz, N_HEADS, nq),
        in_specs=[pl.BlockSpec(memory_space=pltpu.SMEM),
                  pl.BlockSpec((1, 1, 2, LANES, tq), lambda b, h, i: (b, h, 0, 0, i)),
                  pl.BlockSpec((1, seq, LANES), lambda b, h, i: (b, 0, h)),
                  pl.BlockSpec((1, 1, LANES, seq), lambda b, h, i: (b, h, 0, 0)),
                  pl.BlockSpec((1, V_DIM), lambda b, h, i: (0, 0))],
        out_specs=pl.BlockSpec((tq, V_DIM), lambda b, h, i: (b * nq + i, h)),
        scratch_shapes=[pltpu.VMEM((V_DIM + ONES_ROWS, 2 * tq), F32),
                        pltpu.VMEM((1, 2 * tq), F32)],
        compiler_params=_params(("parallel", "parallel", "arbitrary")),
        name="attn_exact" if exact else "attn",
    )(scal, qt, kn3, vt, subln_g.reshape(1, V_DIM))


def _conv_kernel(ca_ref, cb_ref, hca_ref, hcb_ref, dw_ref, dwb_ref, lng_ref, lnb_ref, w_ref, b_ref,
                 o_ref, u_scr, c_scr, sh_scr, *, ts, tiles_per_seq):
    first = (pl.program_id(0) % tiles_per_seq) == 0
    u_scr[CONV_HALO:, :] = ca_ref[...].astype(F32) * _sigmoid(cb_ref[...].astype(F32))
    halo = hca_ref[...].astype(F32) * _sigmoid(hcb_ref[...].astype(F32))
    u_scr[:CONV_HALO, :] = jnp.where(first, 0.0, halo)

    first_off = CONV_HALO - (CONV_TAPS - 1)
    last_off = first_off + CONV_TAPS - 1

    def col_chunk(c, carry):
        cols = pl.ds(pl.multiple_of(c * LANES, LANES), LANES)
        acc = jnp.zeros((ts, LANES), F32)
        for b in range(SUBLANES):
            a_hi = (last_off - b) // SUBLANES
            n = a_hi * SUBLANES + ts
            sh_scr[:n, :] = u_scr[b:b + n, cols]
            for a in range(a_hi + 1):
                t = a * SUBLANES + b - first_off
                if t >= 0:
                    acc = acc + sh_scr[a * SUBLANES:a * SUBLANES + ts, :] * dw_ref[t:t + 1, cols]
        c_scr[:, cols] = acc + dwb_ref[:, cols]
        return carry

    lax.fori_loop(0, c_scr.shape[1] // LANES, col_chunk, 0)

    u = c_scr[...]
    mu = jnp.mean(u, axis=-1, keepdims=True)
    uc = u - mu
    var = jnp.mean(uc * uc, axis=-1, keepdims=True)
    y = uc * lax.rsqrt(var + EPS) * lng_ref[...] + lnb_ref[...]
    y = _silu(y).astype(BF16)
    o_ref[...] = (jnp.dot(y, w_ref[...], preferred_element_type=F32) + b_ref[...]).astype(BF16)


def _conv_branch(proj, conv_dw, conv_dw_b, ln_g, ln_b, w_pw2_bf, b_pw2, seq, ts):
    t = proj.shape[0]
    c = w_pw2_bf.shape[0]
    d = w_pw2_bf.shape[1]
    tiles_per_seq = seq // ts
    hb = ts // CONV_HALO
    dw = jnp.zeros((CONV_HALO, c), F32).at[:CONV_TAPS].set(conv_dw)
    row = lambda a: a.reshape(1, -1)
    return pl.pallas_call(
        functools.partial(_conv_kernel, ts=ts, tiles_per_seq=tiles_per_seq),
        out_shape=jax.ShapeDtypeStruct((t, d), BF16),
        grid=(t // ts,),
        in_specs=[pl.BlockSpec((ts, c), lambda i: (i, 0)),
                  pl.BlockSpec((ts, c), lambda i: (i, 1)),
                  pl.BlockSpec((CONV_HALO, c), lambda i: (jnp.maximum(i * hb - 1, 0), 0)),
                  pl.BlockSpec((CONV_HALO, c), lambda i: (jnp.maximum(i * hb - 1, 0), 1)),
                  pl.BlockSpec((CONV_HALO, c), lambda i: (0, 0)),
                  pl.BlockSpec((1, c), lambda i: (0, 0)),
                  pl.BlockSpec((1, c), lambda i: (0, 0)),
                  pl.BlockSpec((1, c), lambda i: (0, 0)),
                  pl.BlockSpec((c, d), lambda i: (0, 0)),
                  pl.BlockSpec((1, d), lambda i: (0, 0))],
        out_specs=pl.BlockSpec((ts, d), lambda i: (i, 0)),
        scratch_shapes=[pltpu.VMEM((ts + CONV_HALO, c), F32),
                        pltpu.VMEM((ts, c), F32),
                        pltpu.VMEM((ts + CONV_HALO, LANES), F32)],
        compiler_params=_params(("parallel",)),
        name="conv",
    )(proj, proj, proj, proj, dw, row(conv_dw_b), row(ln_g), row(ln_b), w_pw2_bf, row(b_pw2))


def _merge_kernel(x_ref, yc_ref, ya_ref, gc_ref, ga_ref, mod_ref, g2_ref, wo_ref, wr_ref, wsgu_ref, wsd_ref,
                  x1_ref, h2_ref, lg_ref, sh_ref):
    mod = mod_ref[0]
    merged = (_sigmoid(gc_ref[...].astype(F32)) * yc_ref[...].astype(F32)
              + _sigmoid(ga_ref[...].astype(F32)) * ya_ref[...].astype(F32))
    x1 = x_ref[...] + mod[2:3, :] * jnp.dot(merged.astype(BF16), wo_ref[...], preferred_element_type=F32)
    x1_ref[...] = x1
    ms = jnp.mean(x1 * x1, axis=-1, keepdims=True)
    h2 = x1 * lax.rsqrt(ms + EPS) * (g2_ref[...] * (1.0 + mod[4:5, :])) + mod[3:4, :]
    h2b = h2.astype(BF16)
    h2_ref[...] = h2b
    lg_ref[...] = lax.dot_general(wr_ref[...], h2, (((1,), (1,)), ((), ())),
                                  preferred_element_type=F32, precision=lax.Precision.HIGHEST)
    gu = jnp.dot(h2b, wsgu_ref[...], preferred_element_type=F32)
    sd = wsgu_ref.shape[1] // 2
    a = (_silu(gu[:, :sd]) * gu[:, sd:]).astype(BF16)
    sh_ref[...] = jnp.dot(a, wsd_ref[...], preferred_element_type=F32)


def _merge(x2, y_conv, y_attn, proj, mod8, norm2_g, w_out_bf, w_router_t, wsgu_bf, wsd_bf, seq, tm):
    t, d = x2.shape
    tiles_per_seq = seq // tm
    gcol = 5
    tok = lambda i: (i, 0)
    fixed = lambda i: (0, 0)
    return pl.pallas_call(
        _merge_kernel,
        out_shape=(jax.ShapeDtypeStruct((t, d), F32),
                   jax.ShapeDtypeStruct((t, d), BF16),
                   jax.ShapeDtypeStruct((N_EXPERTS, t), F32),
                   jax.ShapeDtypeStruct((t, d), F32)),
        grid=(t // tm,),
        in_specs=[pl.BlockSpec((tm, d), tok),
                  pl.BlockSpec((tm, d), tok),
                  pl.BlockSpec((tm, d), tok),
                  pl.BlockSpec((tm, d), lambda i: (i, gcol)),
                  pl.BlockSpec((tm, d), lambda i: (i, gcol + 1)),
                  pl.BlockSpec((1, 8, d), lambda i: (i // tiles_per_seq, 0, 0)),
                  pl.BlockSpec((1, d), fixed),
                  pl.BlockSpec((d, d), fixed),
                  pl.BlockSpec((N_EXPERTS, d), fixed),
                  pl.BlockSpec(wsgu_bf.shape, fixed),
                  pl.BlockSpec(wsd_bf.shape, fixed)],
        out_specs=(pl.BlockSpec((tm, d), tok),
                   pl.BlockSpec((tm, d), tok),
                   pl.BlockSpec((N_EXPERTS, tm), lambda i: (0, i)),
                   pl.BlockSpec((tm, d), tok)),
        compiler_params=_params(("parallel",)),
        name="merge",
    )(x2, y_conv, y_attn, proj, proj, mod8, norm2_g.reshape(1, d), w_out_bf, w_router_t, wsgu_bf, wsd_bf)


def _route_kernel(lg_ref, bias_ref, tri_ref, low_ref, g_ref, eid_ref, pos_ref, gk_ref, cnt_ref):
    tm = lg_ref.shape[1]
    scores = _sigmoid(lg_ref[...])
    choice = scores + bias_ref[...]
    sub = lax.broadcasted_iota(jnp.int32, (GROUP_SIZE, tm), 0)
    gs = []
    for g in range(N_GROUPS):
        cg = choice[g * GROUP_SIZE:(g + 1) * GROUP_SIZE, :]
        m1 = jnp.max(cg, axis=0, keepdims=True)
        first = jnp.min(jnp.where(cg == m1, sub, GROUP_SIZE), axis=0, keepdims=True)
        m2 = jnp.max(jnp.where(sub == first, -jnp.inf, cg), axis=0, keepdims=True)
        gs.append(m1 + m2)
    masked = []
    for g in range(N_GROUPS):
        rank = jnp.zeros((1, tm), F32)
        for o in range(N_GROUPS):
            if o == g:
                continue
            beats = (gs[o] >= gs[g]) if o < g else (gs[o] > gs[g])
            rank = rank + jnp.where(beats, 1.0, 0.0)
        keep = rank < TOPK_GROUPS
        masked.append(jnp.where(keep, choice[g * GROUP_SIZE:(g + 1) * GROUP_SIZE, :], -jnp.inf))
    masked = jnp.concatenate(masked, axis=0)
    eidx = lax.broadcasted_iota(jnp.int32, (N_EXPERTS, tm), 0)
    rank = jnp.zeros((N_EXPERTS, tm), F32)
    for o in range(N_EXPERTS):
        other = masked[o:o + 1, :]
        tie = jnp.where(eidx > o, 1.0, 0.0)
        rank = rank + jnp.where(other > masked, 1.0, 0.0) + jnp.where(other == masked, tie, 0.0)
    chosen = rank < TOP_K
    w = jnp.where(chosen, scores, 0.0)
    w = w / jnp.sum(w, axis=0, keepdims=True) * ROUTED_SCALE
    wp = jnp.concatenate([w, jnp.zeros((LANES - N_EXPERTS, tm), F32)], axis=0)
    g_ref[...] = wp.T
    onehot = jnp.where(chosen, 1.0, 0.0)
    before = jnp.dot(onehot.astype(BF16), tri_ref[...], preferred_element_type=F32)
    count = jnp.sum(onehot, axis=1, keepdims=True)
    padded = jnp.floor((count + (MOE_ROW_ALIGN - 1)) * (1.0 / MOE_ROW_ALIGN)) * MOE_ROW_ALIGN
    start = jnp.dot(low_ref[...], jnp.broadcast_to(padded, (N_EXPERTS, LANES)).astype(BF16),
                    preferred_element_type=F32)[:, :1]
    pos = start + before
    eid = eidx.astype(F32)
    for k in range(TOP_K):
        mine = jnp.logical_and(chosen, rank == k)
        eid_ref[k:k + 1, :] = jnp.sum(jnp.where(mine, eid, 0.0), axis=0, keepdims=True)
        pos_ref[k:k + 1, :] = jnp.sum(jnp.where(mine, pos, 0.0), axis=0, keepdims=True)
        gk_ref[k:k + 1, :] = jnp.sum(jnp.where(mine, w, 0.0), axis=0, keepdims=True)
    cnt_ref[...] = jnp.broadcast_to(count, cnt_ref.shape)


def _route(logits_t, router_bias, tm):
    t = logits_t.shape[1]
    n = t // tm
    idx = jnp.arange(tm)
    tri = (idx[:, None] < idx[None, :]).astype(BF16)
    e = jnp.arange(N_EXPERTS)
    low = jnp.logical_and(e[:, None] // MOE_EXPERT_BLOCK == e[None, :] // MOE_EXPERT_BLOCK,
                          e[None, :] < e[:, None]).astype(BF16)
    tok = lambda i: (0, i)
    return pl.pallas_call(
        _route_kernel,
        out_shape=(jax.ShapeDtypeStruct((t, LANES), F32),
                   jax.ShapeDtypeStruct((TOP_K, t), F32),
                   jax.ShapeDtypeStruct((TOP_K, t), F32),
                   jax.ShapeDtypeStruct((TOP_K, t), F32),
                   jax.ShapeDtypeStruct((n * N_EXPERTS, LANES), F32)),
        grid=(n,),
        in_specs=[pl.BlockSpec((N_EXPERTS, tm), tok),
                  pl.BlockSpec((N_EXPERTS, 1), lambda i: (0, 0)),
                  pl.BlockSpec((tm, tm), lambda i: (0, 0)),
                  pl.BlockSpec((N_EXPERTS, N_EXPERTS), lambda i: (0, 0))],
        out_specs=(pl.BlockSpec((tm, LANES), lambda i: (i, 0)),
                   pl.BlockSpec((TOP_K, tm), tok),
                   pl.BlockSpec((TOP_K, tm), tok),
                   pl.BlockSpec((TOP_K, tm), tok),
                   pl.BlockSpec((N_EXPERTS, LANES), lambda i: (i, 0))),
        compiler_params=_params(("parallel",)),
        name="route",
    )(logits_t, router_bias.reshape(N_EXPERTS, 1), tri, low)


def _moe_kernel(h2_ref, x1_ref, sh_ref, gates_ref, mod_ref, wgu_ref, wd_ref, o_ref, acc_scr, *, eb):
    e = pl.program_id(1)
    h2 = h2_ref[...]

    @pl.when(e == 0)
    def _():
        acc_scr[...] = sh_ref[...]

    gates = gates_ref[...]
    lane = lax.broadcasted_iota(jnp.int32, gates.shape, 1)
    acc = acc_scr[...]
    for k in range(eb):
        gate = jnp.sum(jnp.where(lane == e * eb + k, gates, 0.0), axis=1, keepdims=True)
        gu = jnp.dot(h2, wgu_ref[k], preferred_element_type=F32)
        a = (_silu(gu[:, :EXPERT_DIM]) * gu[:, EXPERT_DIM:] * gate).astype(BF16)
        acc = acc + jnp.dot(a, wd_ref[k], preferred_element_type=F32)
    acc_scr[...] = acc

    @pl.when(e == pl.num_programs(1) - 1)
    def _():
        o_ref[...] = x1_ref[...] + mod_ref[0][5:6, :] * acc


def _moe(h2, x1, shared, gates, mod8, wgu_bf, wd_bf, seq, tm, eb):
    t, d = x1.shape
    tiles_per_seq = seq // tm
    tok = lambda i, e: (i, 0)
    return pl.pallas_call(
        functools.partial(_moe_kernel, eb=eb),
        out_shape=jax.ShapeDtypeStruct((t, d), F32),
        grid=(t // tm, N_EXPERTS // eb),
        in_specs=[pl.BlockSpec((tm, d), tok),
                  pl.BlockSpec((tm, d), tok),
                  pl.BlockSpec((tm, d), tok),
                  pl.BlockSpec((tm, LANES), tok),
                  pl.BlockSpec((1, 8, d), lambda i, e: (i // tiles_per_seq, 0, 0)),
                  pl.BlockSpec((eb, d, 2 * EXPERT_DIM), lambda i, e: (e, 0, 0)),
                  pl.BlockSpec((eb, EXPERT_DIM, d), lambda i, e: (e, 0, 0))],
        out_specs=pl.BlockSpec((tm, d), tok),
        scratch_shapes=[pltpu.VMEM((tm, d), F32)],
        compiler_params=_params(("parallel", "arbitrary")),
        name="moe_dense",
    )(h2, x1, shared, gates, mod8, wgu_bf, wd_bf)


MOE_WINDOW = 256
MOE_WINDOWS_PER_STEP = 4
MOE_EXPERT_BLOCK = 8
MOE_ROW_ALIGN = 16
MOE_BLOCK_ROWS = (512, 768)
MOE_RUN_ROWS = 64
MOE_RUN_PASSES = 3
MOE_PICK_CHUNK = 256


def _moe_sparse_kernel(start_ref, count_ref, blk_ref, acc_ref, h2_ref, eid_ref, pos_ref, gk_ref, x1_ref, mod_ref,
                       wgu_ref, wd_ref, o_ref, xs_scr, ys_scr, q_scr, *, eb, n_blocks, win, nwin, rows):
    blk = blk_ref[0]
    step = pl.program_id(0)
    run = MOE_RUN_ROWS
    @pl.when(step == 0)
    def _():
        ys_scr[...] = jnp.zeros(ys_scr.shape, BF16)
        xs_scr[:, rows:, :] = jnp.zeros((nwin, xs_scr.shape[1] - rows, xs_scr.shape[2]), BF16)
    for j in range(nwin):
        tok = slice(j * win, (j + 1) * win)
        mine = jnp.floor(eid_ref[:, tok] * (1.0 / eb)) == blk.astype(F32)
        pos = jnp.where(mine, pos_ref[:, tok], -1.0)
        gate = gk_ref[:, tok].astype(BF16)
        for c0 in range(0, rows, MOE_PICK_CHUNK):
            n = min(MOE_PICK_CHUNK, rows - c0)
            row = lax.broadcasted_iota(jnp.int32, (n, win), 0).astype(F32).astype(BF16)
            loc = pos - c0
            loc = jnp.where(jnp.logical_and(loc >= 0, loc < n), loc, -1.0).astype(BF16)
            zero = jnp.zeros((n, win), BF16)
            back = zero
            for k in range(TOP_K):
                back = jnp.where(row == loc[k:k + 1, :], gate[k:k + 1, :], back)
            pick = jnp.where(back != zero, jnp.ones((n, win), BF16), zero)
            q_scr[j, c0:c0 + n, :] = back
            xs_scr[j, c0:c0 + n, :] = jnp.dot(pick, h2_ref[tok, :],
                                              preferred_element_type=F32).astype(BF16)

    def ffn(k, x):
        gu = jnp.dot(x, wgu_ref[k], preferred_element_type=F32)
        a = (_silu(gu[:, :EXPERT_DIM]) * gu[:, EXPERT_DIM:]).astype(BF16)
        return jnp.dot(a, wd_ref[k], preferred_element_type=F32)

    starts, counts = [], []
    for k in range(eb):
        e = blk * eb + k
        starts.append([pl.multiple_of(start_ref[step * nwin + j, e], MOE_ROW_ALIGN) for j in range(nwin)])
        counts.append([count_ref[step * nwin + j, e] for j in range(nwin)])
        y = ffn(k, jnp.concatenate([xs_scr[j, pl.ds(starts[k][j], run), :] for j in range(nwin)], axis=0))
        for j in range(nwin):
            ys_scr[j, pl.ds(starts[k][j], run), :] = y[j * run:(j + 1) * run].astype(BF16)
    for p in range(1, MOE_RUN_PASSES):
        for k in range(eb):
            longest = functools.reduce(jnp.maximum, counts[k])

            @pl.when(longest > p * run)
            def _(k=k, p=p):
                more = [pl.multiple_of(starts[k][j] + p * run, MOE_ROW_ALIGN) for j in range(nwin)]
                y = ffn(k, jnp.concatenate([xs_scr[j, pl.ds(more[j], run), :] for j in range(nwin)], axis=0))
                idx = lax.broadcasted_iota(jnp.int32, (run, y.shape[1]), 0) + p * run
                for j in range(nwin):
                    old = ys_scr[j, pl.ds(more[j], run), :].astype(F32)
                    keep = jnp.where(idx < counts[k][j], y[j * run:(j + 1) * run], old)
                    ys_scr[j, pl.ds(more[j], run), :] = keep.astype(BF16)
    is_last = blk == n_blocks - 1
    gmul = mod_ref[0][5:6, :]
    for j in range(nwin):
        tok = slice(j * win, (j + 1) * win)
        total = acc_ref[tok, :] + lax.dot_general(q_scr[j], ys_scr[j, :rows, :], (((0,), (0,)), ((), ())),
                                                  preferred_element_type=F32)
        o_ref[tok, :] = jnp.where(is_last, x1_ref[tok, :] + gmul * total, total)


def _moe_sparse(h2, x1, shared, eid_k, pos_k, gate_k, run_start, run_count, mod8, wgu_bf, wd_bf, seq, win, nwin, eb,
                rows):
    t, d = x1.shape
    n_eb = N_EXPERTS // eb
    w = win * nwin
    steps_per_seq = seq // w
    tokw = lambda i, *_: (i, 0)
    lists = lambda i, *_: (0, i)
    once = dict(pipeline_mode=pl.Buffered(1))
    grid_spec = pltpu.PrefetchScalarGridSpec(
        num_scalar_prefetch=3,
        grid=(t // w,),
        in_specs=[pl.BlockSpec((w, d), tokw),
                  pl.BlockSpec((w, d), tokw),
                  pl.BlockSpec((TOP_K, w), lists),
                  pl.BlockSpec((TOP_K, w), lists),
                  pl.BlockSpec((TOP_K, w), lists),
                  pl.BlockSpec((w, d), lambda i, s, c, b: (jnp.where(b[0] == n_eb - 1, i, 0), 0), **once),
                  pl.BlockSpec((1, 8, d), lambda i, *_: (i // steps_per_seq, 0, 0)),
                  pl.BlockSpec((eb, d, 2 * EXPERT_DIM), lambda i, s, c, b: (b[0], 0, 0), **once),
                  pl.BlockSpec((eb, EXPERT_DIM, d), lambda i, s, c, b: (b[0], 0, 0), **once)],
        out_specs=pl.BlockSpec((w, d), tokw),
        scratch_shapes=[pltpu.VMEM((nwin, rows + MOE_RUN_PASSES * MOE_RUN_ROWS, d), BF16),
                        pltpu.VMEM((nwin, rows + MOE_RUN_PASSES * MOE_RUN_ROWS, d), BF16),
                        pltpu.VMEM((nwin, rows, win), BF16)])
    call = pl.pallas_call(
        functools.partial(_moe_sparse_kernel, eb=eb, n_blocks=n_eb, win=win, nwin=nwin, rows=rows),
        out_shape=jax.ShapeDtypeStruct((t, d), F32),
        grid_spec=grid_spec,
        compiler_params=_params(("arbitrary",)),
        name="moe_sparse",
    )
    acc = shared
    for b in range(n_eb):
        acc = call(run_start, run_count, jnp.full((1,), b, jnp.int32), acc, h2, eid_k, pos_k, gate_k, x1, mod8,
                   wgu_bf, wd_bf)
    return acc


def _dispatch_tables(count):
    n_win = count.shape[0]
    padded = -(-count // MOE_ROW_ALIGN) * MOE_ROW_ALIGN
    blocks = padded.reshape(n_win, N_EXPERTS // MOE_EXPERT_BLOCK, MOE_EXPERT_BLOCK)
    start = (jnp.cumsum(blocks, axis=-1) - blocks).reshape(n_win, N_EXPERTS)
    return (start.astype(jnp.int32), jnp.max(jnp.sum(blocks, axis=-1)),
            jnp.max(count) <= MOE_RUN_PASSES * MOE_RUN_ROWS)


def _tiles(seq):
    pick = lambda want: min(want, seq)
    return dict(inproj=pick(1024), qk=pick(512), attn=pick(512), conv=pick(256), merge=pick(512),
                route=pick(MOE_WINDOW), moe=pick(1024))


def kernel(x, c, positions, w_ada, b_ada, norm1_g, w_in, conv_dw, conv_dw_b, conv_ln_g, conv_ln_b, w_pw2, b_pw2, q_norm_g, k_norm_g, lambda_q1, lambda_k1, lambda_q2, lambda_k2, subln_g, w_out, norm2_g, w_router, router_bias, w_exp_gu, w_exp_down, w_sh_gu, w_sh_down):
    bsz, seq, d = x.shape
    depth = w_ada.shape[0]
    assert depth == 1 and d == N_HEADS * V_DIM
    tl = _tiles(seq)
    t = bsz * seq
    x2 = x.reshape(t, d)
    pos = positions.astype(F32).reshape(t, 1)
    l = 0

    mod = _adaln(c, w_ada[l], b_ada[l])
    mod8 = jnp.pad(mod.reshape(bsz, 6, d), ((0, 0), (0, 2), (0, 0)))

    proj = _inproj(x2, mod8, norm1_g[l], w_in[l].astype(BF16), seq, tl["inproj"], 1024)

    q_scale = HEAD_DIM ** -0.5 * math.log2(math.e)
    qg = q_norm_g[l] * q_scale
    qg_tab = jnp.tile(qg, 2).reshape(1, LANES)
    kg_tab = jnp.tile(k_norm_g[l], 2).reshape(1, LANES)
    seg_id = jnp.arange(LANES) // HEAD_DIM
    seg_mat = (seg_id[:, None] == seg_id[None, :]).astype(BF16)
    qt, kn, vt = _qkvprep(pos, proj, qg_tab, kg_tab, seg_mat, bsz, seq, tl["qk"])

    lam = (jnp.exp(jnp.sum(lambda_q1[l] * lambda_k1[l])) - jnp.exp(jnp.sum(lambda_q2[l] * lambda_k2[l]))
           + LAMBDA_INIT)
    bound = 1.02 * HEAD_DIM * jnp.max(jnp.abs(qg)) * jnp.max(jnp.abs(k_norm_g[l]))
    scal = jnp.stack([lam, bound]).astype(F32).reshape(1, 2)
    attn_args = (scal, qt, kn.reshape(bsz, seq, -1), vt, subln_g[l], bsz, seq, tl["attn"])
    y_attn = lax.cond(2.0 * bound < SAFE_EXP2_RANGE,
                      lambda: _attention(*attn_args, exact=False),
                      lambda: _attention(*attn_args, exact=True))

    y_conv = _conv_branch(proj, conv_dw[l], conv_dw_b[l], conv_ln_g[l], conv_ln_b[l],
                          w_pw2[l].astype(BF16), b_pw2[l], seq, tl["conv"])

    x1, h2, logits_t, shared = _merge(x2, y_conv, y_attn, proj, mod8, norm2_g[l], w_out[l].astype(BF16),
                                      w_router[l].T, w_sh_gu[l].astype(BF16), w_sh_down[l].astype(BF16),
                                      seq, tl["merge"])
    gates, eid_k, pos_k, gate_k, count = _route(logits_t, router_bias[l], tl["route"])
    wgu_bf = w_exp_gu[l].astype(BF16)
    wd_bf = w_exp_down[l].astype(BF16)
    nwin = min(MOE_WINDOWS_PER_STEP, seq // tl["route"])
    count = count[:, 0].reshape(-1, N_EXPERTS).astype(jnp.int32)
    run_start, block_rows, runs_fit = _dispatch_tables(count)
    branches = [functools.partial(_moe_sparse, h2, x1, shared, eid_k, pos_k, gate_k, run_start, count, mod8,
                                  wgu_bf, wd_bf, seq, tl["route"], nwin, MOE_EXPERT_BLOCK, rows)
                for rows in MOE_BLOCK_ROWS]
    branches.append(lambda: _moe(h2, x1, shared, gates, mod8, wgu_bf, wd_bf, seq, tl["moe"], 4))
    choice = jnp.sum(block_rows > jnp.asarray(MOE_BLOCK_ROWS, jnp.int32))
    out = lax.switch(jnp.where(runs_fit, choice, len(MOE_BLOCK_ROWS)), branches)
    return out.reshape(bsz, seq, d)
```

```python
import functools
import math

import jax
import jax.numpy as jnp
from jax import lax
from jax.experimental import pallas as pl
from jax.experimental.pallas import tpu as pltpu

F32 = jnp.float32
BF16 = jnp.bfloat16
QK_DTYPE = jnp.float8_e4m3fn

EPS = 1e-6
CONV_TAPS = 31
CONV_HALO = 32
N_HEADS = 8
HEAD_DIM = 64
V_DIM = 2 * HEAD_DIM
ROPE_THETA = 500000.0
ROPE_DIM = HEAD_DIM // 4
N_EXPERTS = 64
N_GROUPS = 8
GROUP_SIZE = N_EXPERTS // N_GROUPS
TOPK_GROUPS = 4
TOP_K = 8
EXPERT_DIM = 256
ROUTED_SCALE = 2.5
LAMBDA_INIT = 0.8 - 0.6 * math.exp(-0.3 * 0)
LANES = 128
SUBLANES = 8
NEG_BIG = -1e30
VMEM_LIMIT = 56 * 1024 * 1024
SAFE_EXP2_RANGE = 100.0


def _params(sem):
    return pltpu.CompilerParams(dimension_semantics=sem, vmem_limit_bytes=VMEM_LIMIT)


def _sigmoid(x):
    return 1.0 / (1.0 + jnp.exp(-x))


def _silu(x):
    return x * _sigmoid(x)


def _adaln_kernel(c_ref, w_ref, b_ref, o_ref):
    c = c_ref[...]
    o_ref[...] = jnp.dot(_silu(c), w_ref[...], preferred_element_type=F32,
                         precision=lax.Precision.HIGHEST) + b_ref[...]


def _adaln(c, w_ada, b_ada):
    bsz, d = c.shape
    n = w_ada.shape[1]
    tn = 1536
    return pl.pallas_call(
        _adaln_kernel,
        out_shape=jax.ShapeDtypeStruct((bsz, n), F32),
        grid=(n // tn,),
        in_specs=[pl.BlockSpec((bsz, d), lambda j: (0, 0)),
                  pl.BlockSpec((d, tn), lambda j: (0, j)),
                  pl.BlockSpec((1, tn), lambda j: (0, j))],
        out_specs=pl.BlockSpec((bsz, tn), lambda j: (0, j)),
        compiler_params=_params(("parallel",)),
        name="adaln",
    )(c, w_ada, b_ada.reshape(1, n))


def _inproj_kernel(x_ref, mod_ref, g_ref, w_ref, o_ref, h_scr):
    @pl.when(pl.program_id(1) == 0)
    def _():
        x = x_ref[...]
        ms = jnp.mean(x * x, axis=-1, keepdims=True)
        mod = mod_ref[0]
        gmul = g_ref[...] * (1.0 + mod[1:2, :])
        h_scr[...] = (x * lax.rsqrt(ms + EPS) * gmul + mod[0:1, :]).astype(BF16)

    o_ref[...] = jnp.dot(h_scr[...], w_ref[...], preferred_element_type=F32).astype(BF16)


def _inproj(x2, mod8, norm_g, w_in_bf, seq, tm, tn):
    t, d = x2.shape
    n = w_in_bf.shape[1]
    tiles_per_seq = seq // tm
    return pl.pallas_call(
        _inproj_kernel,
        out_shape=jax.ShapeDtypeStruct((t, n), BF16),
        grid=(t // tm, n // tn),
        in_specs=[pl.BlockSpec((tm, d), lambda i, j: (i, 0)),
                  pl.BlockSpec((1, 8, d), lambda i, j: (i // tiles_per_seq, 0, 0)),
                  pl.BlockSpec((1, d), lambda i, j: (0, 0)),
                  pl.BlockSpec((d, tn), lambda i, j: (0, j))],
        out_specs=pl.BlockSpec((tm, tn), lambda i, j: (i, j)),
        scratch_shapes=[pltpu.VMEM((tm, d), BF16)],
        compiler_params=_params(("parallel", "arbitrary")),
        name="inproj",
    )(x2, mod8, norm_g.reshape(1, d), w_in_bf)


def _qkvprep_kernel(pos_ref, q_ref, k_ref, v_ref, qg_ref, kg_ref, seg_ref, qt_ref, ko_ref, vt_ref):
    pos = pos_ref[...]
    lane = lax.broadcasted_iota(jnp.int32, (1, LANES), 1)
    in_seg = lane % HEAD_DIM
    half = ROPE_DIM // 2
    freq_idx = (in_seg % half).astype(F32)
    inv_freq = jnp.exp(freq_idx * (-2.0 * math.log(ROPE_THETA) / ROPE_DIM))
    ang = pos * inv_freq
    cos = jnp.cos(ang)
    sin = jnp.sin(ang)
    is_lo = in_seg < half
    is_hi = jnp.logical_and(in_seg >= half, in_seg < ROPE_DIM)
    c_tab = jnp.where(in_seg < ROPE_DIM, cos, 1.0)
    s_lo = jnp.where(is_lo, -sin, 0.0)
    s_hi = jnp.where(is_hi, sin, 0.0)
    seg = seg_ref[...]
    first_half = lane < HEAD_DIM

    def prep(x, g):
        sq = x * x
        hi = sq.astype(BF16)
        lo = (sq - hi.astype(F32)).astype(BF16)
        ssq = (jnp.dot(hi, seg, preferred_element_type=F32)
               + jnp.dot(lo, seg, preferred_element_type=F32))
        xn = x * lax.rsqrt(ssq * (1.0 / HEAD_DIM) + EPS) * g
        up = pltpu.roll(xn, LANES - half, axis=1)
        dn = pltpu.roll(xn, half, axis=1)
        return xn * c_tab + up * s_lo + dn * s_hi

    for h in range(N_HEADS):
        cols = slice(h * LANES, (h + 1) * LANES)
        qr = prep(q_ref[:, cols].astype(F32), qg_ref[...])
        qt_ref[0, h, 0] = jnp.where(first_half, qr, 0.0).T.astype(qt_ref.dtype)
        qt_ref[0, h, 1] = jnp.where(first_half, 0.0, qr).T.astype(qt_ref.dtype)
        ko_ref[:, cols] = prep(k_ref[:, cols].astype(F32), kg_ref[...]).astype(ko_ref.dtype)
        vt_ref[0, h, :LANES, :] = v_ref[:, cols].astype(F32).T.astype(BF16)
        vt_ref[0, h, LANES:, :] = jnp.ones((ONES_ROWS, v_ref.shape[0]), BF16)


def _qkvprep(pos, proj, qg_tab, kg_tab, seg_mat, bsz, seq, tm, qk_dtype):
    t = proj.shape[0]
    d = N_HEADS * LANES
    tiles_per_seq = seq // tm
    qcol = 2
    return pl.pallas_call(
        _qkvprep_kernel,
        out_shape=(jax.ShapeDtypeStruct((bsz, N_HEADS, 2, LANES, seq), qk_dtype),
                   jax.ShapeDtypeStruct((t, d), qk_dtype),
                   jax.ShapeDtypeStruct((bsz, N_HEADS, LANES + ONES_ROWS, seq), BF16)),
        grid=(t // tm,),
        in_specs=[pl.BlockSpec((tm, 1), lambda i: (i, 0)),
                  pl.BlockSpec((tm, d), lambda i: (i, qcol)),
                  pl.BlockSpec((tm, d), lambda i: (i, qcol + 1)),
                  pl.BlockSpec((tm, d), lambda i: (i, qcol + 2)),
                  pl.BlockSpec((1, LANES), lambda i: (0, 0)),
                  pl.BlockSpec((1, LANES), lambda i: (0, 0)),
                  pl.BlockSpec((LANES, LANES), lambda i: (0, 0))],
        out_specs=(pl.BlockSpec((1, N_HEADS, 2, LANES, tm),
                                lambda i: (i // tiles_per_seq, 0, 0, 0, i % tiles_per_seq)),
                   pl.BlockSpec((tm, d), lambda i: (i, 0)),
                   pl.BlockSpec((1, N_HEADS, LANES + ONES_ROWS, tm),
                                lambda i: (i // tiles_per_seq, 0, 0, i % tiles_per_seq))),
        compiler_params=_params(("parallel",)),
        name="qkvprep",
    )(pos, proj, proj, proj, qg_tab, kg_tab, seg_mat)


ONES_ROWS = 16
KV_UNROLL = 4


def _attn_kernel(sc_ref, qt_ref, k_ref, vt_ref, sg_ref, o_ref, acc_scr, m_scr, *, tq, exact):
    qi = pl.program_id(2)
    lam = sc_ref[0, 0]
    shift = sc_ref[0, 1]
    acc_scr[...] = jnp.zeros(acc_scr.shape, F32)
    if exact:
        m_scr[...] = jnp.full(m_scr.shape, NEG_BIG, F32)

    def step(j, masked):
        off = pl.multiple_of(j * tq, tq)
        kj = k_ref[0, pl.ds(off, tq), :]
        s = jnp.concatenate([jnp.dot(kj, qt_ref[0, 0, 0], preferred_element_type=F32),
                             jnp.dot(kj, qt_ref[0, 0, 1], preferred_element_type=F32)], axis=1)
        if masked:
            key = lax.broadcasted_iota(jnp.int32, (tq, 2 * tq), 0)
            qry = lax.broadcasted_iota(jnp.int32, (tq, 2 * tq), 1) % tq
            s = jnp.where(key <= qry, s, NEG_BIG)
        vx = vt_ref[0, 0, :, pl.ds(off, tq)]
        if exact:
            m_prev = m_scr[...]
            m_next = jnp.maximum(m_prev, jnp.max(s, axis=0, keepdims=True))
            p = jnp.exp2(s - m_next).astype(BF16)
            acc_scr[...] = (acc_scr[...] * jnp.exp2(m_prev - m_next)
                            + jnp.dot(vx, p, preferred_element_type=F32))
            m_scr[...] = m_next
            return None
        p = jnp.exp2(s.astype(BF16))
        return jnp.dot(vx, p, preferred_element_type=F32)

    def add(parts):
        if parts[0] is not None:
            while len(parts) > 1:
                parts = [a + b for a, b in zip(parts[::2], parts[1::2])] + parts[len(parts) // 2 * 2:]
            acc_scr[...] += parts[0]

    def body(g, carry):
        add([step(g * KV_UNROLL + u, False) for u in range(KV_UNROLL)])
        return carry

    n_groups = qi // KV_UNROLL
    lax.fori_loop(0, n_groups, body, 0)

    def tail(j, carry):
        add([step(j, False)])
        return carry

    lax.fori_loop(n_groups * KV_UNROLL, qi, tail, 0)
    add([step(qi, True)])

    acc = acc_scr[...]
    o = acc[:V_DIM, :] / acc[V_DIM:V_DIM + 1, :]
    o = o[:, :tq] - lam * o[:, tq:]
    ms = jnp.mean(o * o, axis=0, keepdims=True)
    y = (o * lax.rsqrt(ms + EPS)).T
    o_ref[...] = (y * (sg_ref[...] * (1.0 - LAMBDA_INIT))).astype(BF16)


def _attention(scal, qt, kn3, vt, subln_g, bsz, seq, tq, exact):
    nq = seq // tq
    return pl.pallas_call(
        functools.partial(_attn_kernel, tq=tq, exact=exact),
        out_shape=jax.ShapeDtypeStruct((bsz * seq, N_HEADS * V_DIM), BF16),
        grid=(bsz, N_HEADS, nq),
        in_specs=[pl.BlockSpec(memory_space=pltpu.SMEM),
                  pl.BlockSpec((1, 1, 2, LANES, tq), lambda b, h, i: (b, h, 0, 0, i)),
                  pl.BlockSpec((1, seq, LANES), lambda b, h, i: (b, 0, h)),
                  pl.BlockSpec((1, 1, LANES + ONES_ROWS, seq), lambda b, h, i: (b, h, 0, 0)),
                  pl.BlockSpec((1, V_DIM), lambda b, h, i: (0, 0))],
        out_specs=pl.BlockSpec((tq, V_DIM), lambda b, h, i: (b * nq + i, h)),
        scratch_shapes=[pltpu.VMEM((V_DIM + ONES_ROWS, 2 * tq), F32),
                        pltpu.VMEM((1, 2 * tq), F32)],
        compiler_params=_params(("parallel", "parallel", "arbitrary")),
        name="attn_exact" if exact else "attn",
    )(scal, qt, kn3, vt, subln_g.reshape(1, V_DIM))


def _conv_kernel(ca_ref, cb_ref, hca_ref, hcb_ref, dw_ref, dwb_ref, lng_ref, lnb_ref, w_ref, b_ref,
                 o_ref, u_scr, c_scr, sh_scr, *, ts, tiles_per_seq):
    first = (pl.program_id(0) % tiles_per_seq) == 0
    u_scr[CONV_HALO:, :] = ca_ref[...].astype(F32) * _sigmoid(cb_ref[...].astype(F32))
    halo = hca_ref[...].astype(F32) * _sigmoid(hcb_ref[...].astype(F32))
    u_scr[:CONV_HALO, :] = jnp.where(first, 0.0, halo)

    first_off = CONV_HALO - (CONV_TAPS - 1)
    last_off = first_off + CONV_TAPS - 1

    def col_chunk(c, carry):
        cols = pl.ds(pl.multiple_of(c * LANES, LANES), LANES)
        acc = jnp.zeros((ts, LANES), F32)
        for b in range(SUBLANES):
            a_hi = (last_off - b) // SUBLANES
            n = a_hi * SUBLANES + ts
            sh_scr[:n, :] = u_scr[b:b + n, cols]
            for a in range(a_hi + 1):
                t = a * SUBLANES + b - first_off
                if t >= 0:
                    acc = acc + sh_scr[a * SUBLANES:a * SUBLANES + ts, :] * dw_ref[t:t + 1, cols]
        c_scr[:, cols] = acc + dwb_ref[:, cols]
        return carry

    lax.fori_loop(0, c_scr.shape[1] // LANES, col_chunk, 0)

    u = c_scr[...]
    mu = jnp.mean(u, axis=-1, keepdims=True)
    uc = u - mu
    var = jnp.mean(uc * uc, axis=-1, keepdims=True)
    y = uc * lax.rsqrt(var + EPS) * lng_ref[...] + lnb_ref[...]
    y = _silu(y).astype(BF16)
    o_ref[...] = (jnp.dot(y, w_ref[...], preferred_element_type=F32) + b_ref[...]).astype(BF16)


def _conv_branch(proj, conv_dw, conv_dw_b, ln_g, ln_b, w_pw2_bf, b_pw2, seq, ts):
    t = proj.shape[0]
    c = w_pw2_bf.shape[0]
    d = w_pw2_bf.shape[1]
    tiles_per_seq = seq // ts
    hb = ts // CONV_HALO
    dw = jnp.zeros((CONV_HALO, c), F32).at[:CONV_TAPS].set(conv_dw)
    row = lambda a: a.reshape(1, -1)
    return pl.pallas_call(
        functools.partial(_conv_kernel, ts=ts, tiles_per_seq=tiles_per_seq),
        out_shape=jax.ShapeDtypeStruct((t, d), BF16),
        grid=(t // ts,),
        in_specs=[pl.BlockSpec((ts, c), lambda i: (i, 0)),
                  pl.BlockSpec((ts, c), lambda i: (i, 1)),
                  pl.BlockSpec((CONV_HALO, c), lambda i: (jnp.maximum(i * hb - 1, 0), 0)),
                  pl.BlockSpec((CONV_HALO, c), lambda i: (jnp.maximum(i * hb - 1, 0), 1)),
                  pl.BlockSpec((CONV_HALO, c), lambda i: (0, 0)),
                  pl.BlockSpec((1, c), lambda i: (0, 0)),
                  pl.BlockSpec((1, c), lambda i: (0, 0)),
                  pl.BlockSpec((1, c), lambda i: (0, 0)),
                  pl.BlockSpec((c, d), lambda i: (0, 0)),
                  pl.BlockSpec((1, d), lambda i: (0, 0))],
        out_specs=pl.BlockSpec((ts, d), lambda i: (i, 0)),
        scratch_shapes=[pltpu.VMEM((ts + CONV_HALO, c), F32),
                        pltpu.VMEM((ts, c), F32),
                        pltpu.VMEM((ts + CONV_HALO, LANES), F32)],
        compiler_params=_params(("parallel",)),
        name="conv",
    )(proj, proj, proj, proj, dw, row(conv_dw_b), row(ln_g), row(ln_b), w_pw2_bf, row(b_pw2))


def _merge_kernel(x_ref, yc_ref, ya_ref, gc_ref, ga_ref, mod_ref, g2_ref, wo_ref, wr_ref, wsgu_ref, wsd_ref,
                  x1_ref, h2_ref, lg_ref, sh_ref):
    mod = mod_ref[0]
    merged = (_sigmoid(gc_ref[...].astype(F32)) * yc_ref[...].astype(F32)
              + _sigmoid(ga_ref[...].astype(F32)) * ya_ref[...].astype(F32))
    x1 = x_ref[...] + mod[2:3, :] * jnp.dot(merged.astype(BF16), wo_ref[...], preferred_element_type=F32)
    x1_ref[...] = x1
    ms = jnp.mean(x1 * x1, axis=-1, keepdims=True)
    h2 = x1 * lax.rsqrt(ms + EPS) * (g2_ref[...] * (1.0 + mod[4:5, :])) + mod[3:4, :]
    h2b = h2.astype(BF16)
    h2_ref[...] = h2b
    lg_ref[...] = lax.dot_general(wr_ref[...], h2, (((1,), (1,)), ((), ())),
                                  preferred_element_type=F32, precision=lax.Precision.HIGHEST)
    gu = jnp.dot(h2b, wsgu_ref[...], preferred_element_type=F32)
    sd = wsgu_ref.shape[1] // 2
    a = (_silu(gu[:, :sd]) * gu[:, sd:]).astype(BF16)
    sh_ref[...] = jnp.dot(a, wsd_ref[...], preferred_element_type=F32)


def _merge(x2, y_conv, y_attn, proj, mod8, norm2_g, w_out_bf, w_router_t, wsgu_bf, wsd_bf, seq, tm):
    t, d = x2.shape
    tiles_per_seq = seq // tm
    gcol = 5
    tok = lambda i: (i, 0)
    fixed = lambda i: (0, 0)
    return pl.pallas_call(
        _merge_kernel,
        out_shape=(jax.ShapeDtypeStruct((t, d), F32),
                   jax.ShapeDtypeStruct((t, d), BF16),
                   jax.ShapeDtypeStruct((N_EXPERTS, t), F32),
                   jax.ShapeDtypeStruct((t, d), F32)),
        grid=(t // tm,),
        in_specs=[pl.BlockSpec((tm, d), tok),
                  pl.BlockSpec((tm, d), tok),
                  pl.BlockSpec((tm, d), tok),
                  pl.BlockSpec((tm, d), lambda i: (i, gcol)),
                  pl.BlockSpec((tm, d), lambda i: (i, gcol + 1)),
                  pl.BlockSpec((1, 8, d), lambda i: (i // tiles_per_seq, 0, 0)),
                  pl.BlockSpec((1, d), fixed),
                  pl.BlockSpec((d, d), fixed),
                  pl.BlockSpec((N_EXPERTS, d), fixed),
                  pl.BlockSpec(wsgu_bf.shape, fixed),
                  pl.BlockSpec(wsd_bf.shape, fixed)],
        out_specs=(pl.BlockSpec((tm, d), tok),
                   pl.BlockSpec((tm, d), tok),
                   pl.BlockSpec((N_EXPERTS, tm), lambda i: (0, i)),
                   pl.BlockSpec((tm, d), tok)),
        compiler_params=_params(("parallel",)),
        name="merge",
    )(x2, y_conv, y_attn, proj, proj, mod8, norm2_g.reshape(1, d), w_out_bf, w_router_t, wsgu_bf, wsd_bf)


def _route_kernel(lg_ref, bias_ref, tri_ref, low_ref, g_ref, eid_ref, pos_ref, gk_ref, cnt_ref):
    tm = lg_ref.shape[1]
    scores = _sigmoid(lg_ref[...])
    choice = scores + bias_ref[...]
    sub = lax.broadcasted_iota(jnp.int32, (GROUP_SIZE, tm), 0)
    gs = []
    for g in range(N_GROUPS):
        cg = choice[g * GROUP_SIZE:(g + 1) * GROUP_SIZE, :]
        m1 = jnp.max(cg, axis=0, keepdims=True)
        first = jnp.min(jnp.where(cg == m1, sub, GROUP_SIZE), axis=0, keepdims=True)
        m2 = jnp.max(jnp.where(sub == first, -jnp.inf, cg), axis=0, keepdims=True)
        gs.append(m1 + m2)
    masked = []
    for g in range(N_GROUPS):
        rank = jnp.zeros((1, tm), F32)
        for o in range(N_GROUPS):
            if o == g:
                continue
            beats = (gs[o] >= gs[g]) if o < g else (gs[o] > gs[g])
            rank = rank + jnp.where(beats, 1.0, 0.0)
        keep = rank < TOPK_GROUPS
        masked.append(jnp.where(keep, choice[g * GROUP_SIZE:(g + 1) * GROUP_SIZE, :], -jnp.inf))
    masked = jnp.concatenate(masked, axis=0)
    eidx = lax.broadcasted_iota(jnp.int32, (N_EXPERTS, tm), 0)
    rank = jnp.zeros((N_EXPERTS, tm), F32)
    for o in range(N_EXPERTS):
        other = masked[o:o + 1, :]
        tie = jnp.where(eidx > o, 1.0, 0.0)
        rank = rank + jnp.where(other > masked, 1.0, 0.0) + jnp.where(other == masked, tie, 0.0)
    chosen = rank < TOP_K
    w = jnp.where(chosen, scores, 0.0)
    w = w / jnp.sum(w, axis=0, keepdims=True) * ROUTED_SCALE
    wp = jnp.concatenate([w, jnp.zeros((LANES - N_EXPERTS, tm), F32)], axis=0)
    g_ref[...] = wp.T
    onehot = jnp.where(chosen, 1.0, 0.0)
    before = jnp.dot(onehot.astype(BF16), tri_ref[...], preferred_element_type=F32)
    count = jnp.sum(onehot, axis=1, keepdims=True)
    padded = jnp.floor((count + (MOE_ROW_ALIGN - 1)) * (1.0 / MOE_ROW_ALIGN)) * MOE_ROW_ALIGN
    start = jnp.dot(low_ref[...], jnp.broadcast_to(padded, (N_EXPERTS, LANES)).astype(BF16),
                    preferred_element_type=F32)[:, :1]
    pos = start + before
    eid = eidx.astype(F32)
    for k in range(TOP_K):
        mine = jnp.logical_and(chosen, rank == k)
        eid_ref[k:k + 1, :] = jnp.sum(jnp.where(mine, eid, 0.0), axis=0, keepdims=True)
        pos_ref[k:k + 1, :] = jnp.sum(jnp.where(mine, pos, 0.0), axis=0, keepdims=True)
        gk_ref[k:k + 1, :] = jnp.sum(jnp.where(mine, w, 0.0), axis=0, keepdims=True)
    cnt_ref[...] = jnp.broadcast_to(count, cnt_ref.shape)


def _route(logits_t, router_bias, tm):
    t = logits_t.shape[1]
    n = t // tm
    idx = jnp.arange(tm)
    tri = (idx[:, None] < idx[None, :]).astype(BF16)
    e = jnp.arange(N_EXPERTS)
    low = jnp.logical_and(e[:, None] // MOE_EXPERT_BLOCK == e[None, :] // MOE_EXPERT_BLOCK,
                          e[None, :] < e[:, None]).astype(BF16)
    tok = lambda i: (0, i)
    return pl.pallas_call(
        _route_kernel,
        out_shape=(jax.ShapeDtypeStruct((t, LANES), F32),
                   jax.ShapeDtypeStruct((TOP_K, t), F32),
                   jax.ShapeDtypeStruct((TOP_K, t), F32),
                   jax.ShapeDtypeStruct((TOP_K, t), F32),
                   jax.ShapeDtypeStruct((n * N_EXPERTS, LANES), F32)),
        grid=(n,),
        in_specs=[pl.BlockSpec((N_EXPERTS, tm), tok),
                  pl.BlockSpec((N_EXPERTS, 1), lambda i: (0, 0)),
                  pl.BlockSpec((tm, tm), lambda i: (0, 0)),
                  pl.BlockSpec((N_EXPERTS, N_EXPERTS), lambda i: (0, 0))],
        out_specs=(pl.BlockSpec((tm, LANES), lambda i: (i, 0)),
                   pl.BlockSpec((TOP_K, tm), tok),
                   pl.BlockSpec((TOP_K, tm), tok),
                   pl.BlockSpec((TOP_K, tm), tok),
                   pl.BlockSpec((N_EXPERTS, LANES), lambda i: (i, 0))),
        compiler_params=_params(("parallel",)),
        name="route",
    )(logits_t, router_bias.reshape(N_EXPERTS, 1), tri, low)


def _moe_kernel(h2_ref, x1_ref, sh_ref, gates_ref, mod_ref, wgu_ref, wd_ref, o_ref, acc_scr, *, eb):
    e = pl.program_id(1)
    h2 = h2_ref[...]

    @pl.when(e == 0)
    def _():
        acc_scr[...] = sh_ref[...]

    gates = gates_ref[...]
    lane = lax.broadcasted_iota(jnp.int32, gates.shape, 1)
    acc = acc_scr[...]
    for k in range(eb):
        gate = jnp.sum(jnp.where(lane == e * eb + k, gates, 0.0), axis=1, keepdims=True)
        gu = jnp.dot(h2, wgu_ref[k], preferred_element_type=F32)
        a = (_silu(gu[:, :EXPERT_DIM]) * gu[:, EXPERT_DIM:] * gate).astype(BF16)
        acc = acc + jnp.dot(a, wd_ref[k], preferred_element_type=F32)
    acc_scr[...] = acc

    @pl.when(e == pl.num_programs(1) - 1)
    def _():
        o_ref[...] = x1_ref[...] + mod_ref[0][5:6, :] * acc


def _moe(h2, x1, shared, gates, mod8, wgu_bf, wd_bf, seq, tm, eb):
    t, d = x1.shape
    tiles_per_seq = seq // tm
    tok = lambda i, e: (i, 0)
    return pl.pallas_call(
        functools.partial(_moe_kernel, eb=eb),
        out_shape=jax.ShapeDtypeStruct((t, d), F32),
        grid=(t // tm, N_EXPERTS // eb),
        in_specs=[pl.BlockSpec((tm, d), tok),
                  pl.BlockSpec((tm, d), tok),
                  pl.BlockSpec((tm, d), tok),
                  pl.BlockSpec((tm, LANES), tok),
                  pl.BlockSpec((1, 8, d), lambda i, e: (i // tiles_per_seq, 0, 0)),
                  pl.BlockSpec((eb, d, 2 * EXPERT_DIM), lambda i, e: (e, 0, 0)),
                  pl.BlockSpec((eb, EXPERT_DIM, d), lambda i, e: (e, 0, 0))],
        out_specs=pl.BlockSpec((tm, d), tok),
        scratch_shapes=[pltpu.VMEM((tm, d), F32)],
        compiler_params=_params(("parallel", "arbitrary")),
        name="moe_dense",
    )(h2, x1, shared, gates, mod8, wgu_bf, wd_bf)


MOE_WINDOW = 256
MOE_WINDOWS_PER_STEP = 4
MOE_EXPERT_BLOCK = 8
MOE_ROW_ALIGN = 16
MOE_BLOCK_ROWS = 768
MOE_RUN_ROWS = 64
MOE_RUN_PASSES = 3
MOE_PICK_CHUNK = 256


def _moe_sparse_kernel(start_ref, count_ref, blk_ref, acc_ref, h2_ref, eid_ref, pos_ref, gk_ref, x1_ref, mod_ref,
                       wgu_ref, wd_ref, o_ref, xs_scr, ys_scr, q_scr, *, eb, n_blocks, win, nwin, rows):
    blk = blk_ref[0]
    step = pl.program_id(0)
    run = MOE_RUN_ROWS
    @pl.when(step == 0)
    def _():
        ys_scr[...] = jnp.zeros(ys_scr.shape, BF16)
        xs_scr[:, rows:, :] = jnp.zeros((nwin, xs_scr.shape[1] - rows, xs_scr.shape[2]), BF16)
    for j in range(nwin):
        tok = slice(j * win, (j + 1) * win)
        mine = jnp.floor(eid_ref[:, tok] * (1.0 / eb)) == blk.astype(F32)
        pos = jnp.where(mine, pos_ref[:, tok], -1.0)
        gate = gk_ref[:, tok].astype(BF16)
        for c0 in range(0, rows, MOE_PICK_CHUNK):
            n = min(MOE_PICK_CHUNK, rows - c0)
            row = lax.broadcasted_iota(jnp.int32, (n, win), 0).astype(F32).astype(BF16)
            loc = pos - c0
            loc = jnp.where(jnp.logical_and(loc >= 0, loc < n), loc, -1.0).astype(BF16)
            zero = jnp.zeros((n, win), BF16)
            back = zero
            for k in range(TOP_K):
                back = jnp.where(row == loc[k:k + 1, :], gate[k:k + 1, :], back)
            pick = jnp.where(back != zero, jnp.ones((n, win), BF16), zero)
            q_scr[j, c0:c0 + n, :] = back
            xs_scr[j, c0:c0 + n, :] = jnp.dot(pick, h2_ref[tok, :],
                                              preferred_element_type=F32).astype(BF16)

    def ffn(k, x):
        gu = jnp.dot(x, wgu_ref[k], preferred_element_type=F32)
        a = (_silu(gu[:, :EXPERT_DIM]) * gu[:, EXPERT_DIM:]).astype(BF16)
        return jnp.dot(a, wd_ref[k], preferred_element_type=F32)

    starts, counts = [], []
    for k in range(eb):
        e = blk * eb + k
        starts.append([pl.multiple_of(start_ref[step * nwin + j, e], MOE_ROW_ALIGN) for j in range(nwin)])
        counts.append([count_ref[step * nwin + j, e] for j in range(nwin)])
        y = ffn(k, jnp.concatenate([xs_scr[j, pl.ds(starts[k][j], run), :] for j in range(nwin)], axis=0))
        for j in range(nwin):
            ys_scr[j, pl.ds(starts[k][j], run), :] = y[j * run:(j + 1) * run].astype(BF16)
    for p in range(1, MOE_RUN_PASSES):
        for k in range(eb):
            longest = functools.reduce(jnp.maximum, counts[k])

            @pl.when(longest > p * run)
            def _(k=k, p=p):
                more = [pl.multiple_of(starts[k][j] + p * run, MOE_ROW_ALIGN) for j in range(nwin)]
                y = ffn(k, jnp.concatenate([xs_scr[j, pl.ds(more[j], run), :] for j in range(nwin)], axis=0))
                idx = lax.broadcasted_iota(jnp.int32, (run, y.shape[1]), 0) + p * run
                for j in range(nwin):
                    old = ys_scr[j, pl.ds(more[j], run), :].astype(F32)
                    keep = jnp.where(idx < counts[k][j], y[j * run:(j + 1) * run], old)
                    ys_scr[j, pl.ds(more[j], run), :] = keep.astype(BF16)
    is_last = blk == n_blocks - 1
    gmul = mod_ref[0][5:6, :]
    for j in range(nwin):
        tok = slice(j * win, (j + 1) * win)
        total = acc_ref[tok, :] + lax.dot_general(q_scr[j], ys_scr[j, :rows, :], (((0,), (0,)), ((), ())),
                                                  preferred_element_type=F32)
        o_ref[tok, :] = jnp.where(is_last, x1_ref[tok, :] + gmul * total, total)


def _moe_sparse(h2, x1, shared, eid_k, pos_k, gate_k, run_start, run_count, mod8, wgu_bf, wd_bf, seq, win, nwin, eb):
    t, d = x1.shape
    n_eb = N_EXPERTS // eb
    w = win * nwin
    steps_per_seq = seq // w
    rows = MOE_BLOCK_ROWS
    tokw = lambda i, *_: (i, 0)
    lists = lambda i, *_: (0, i)
    once = dict(pipeline_mode=pl.Buffered(1))
    grid_spec = pltpu.PrefetchScalarGridSpec(
        num_scalar_prefetch=3,
        grid=(t // w,),
        in_specs=[pl.BlockSpec((w, d), tokw),
                  pl.BlockSpec((w, d), tokw),
                  pl.BlockSpec((TOP_K, w), lists),
                  pl.BlockSpec((TOP_K, w), lists),
                  pl.BlockSpec((TOP_K, w), lists),
                  pl.BlockSpec((w, d), lambda i, s, c, b: (jnp.where(b[0] == n_eb - 1, i, 0), 0), **once),
                  pl.BlockSpec((1, 8, d), lambda i, *_: (i // steps_per_seq, 0, 0)),
                  pl.BlockSpec((eb, d, 2 * EXPERT_DIM), lambda i, s, c, b: (b[0], 0, 0), **once),
                  pl.BlockSpec((eb, EXPERT_DIM, d), lambda i, s, c, b: (b[0], 0, 0), **once)],
        out_specs=pl.BlockSpec((w, d), tokw),
        scratch_shapes=[pltpu.VMEM((nwin, rows + MOE_RUN_PASSES * MOE_RUN_ROWS, d), BF16),
                        pltpu.VMEM((nwin, rows + MOE_RUN_PASSES * MOE_RUN_ROWS, d), BF16),
                        pltpu.VMEM((nwin, rows, win), BF16)])
    call = pl.pallas_call(
        functools.partial(_moe_sparse_kernel, eb=eb, n_blocks=n_eb, win=win, nwin=nwin, rows=rows),
        out_shape=jax.ShapeDtypeStruct((t, d), F32),
        grid_spec=grid_spec,
        compiler_params=_params(("arbitrary",)),
        name="moe_sparse",
    )
    acc = shared
    for b in range(n_eb):
        acc = call(run_start, run_count, jnp.full((1,), b, jnp.int32), acc, h2, eid_k, pos_k, gate_k, x1, mod8,
                   wgu_bf, wd_bf)
    return acc


def _dispatch_tables(count):
    n_win = count.shape[0]
    padded = -(-count // MOE_ROW_ALIGN) * MOE_ROW_ALIGN
    blocks = padded.reshape(n_win, N_EXPERTS // MOE_EXPERT_BLOCK, MOE_EXPERT_BLOCK)
    start = (jnp.cumsum(blocks, axis=-1) - blocks).reshape(n_win, N_EXPERTS)
    fits = jnp.logical_and(jnp.max(count) <= MOE_RUN_PASSES * MOE_RUN_ROWS, jnp.max(jnp.sum(blocks, axis=-1)) <= MOE_BLOCK_ROWS)
    return start.astype(jnp.int32), fits


def _tiles(seq):
    pick = lambda want: min(want, seq)
    return dict(inproj=pick(1024), qk=pick(512), attn=pick(512), conv=pick(256), merge=pick(512),
                route=pick(MOE_WINDOW), moe=pick(1024))


def kernel(x, c, positions, w_ada, b_ada, norm1_g, w_in, conv_dw, conv_dw_b, conv_ln_g, conv_ln_b, w_pw2, b_pw2, q_norm_g, k_norm_g, lambda_q1, lambda_k1, lambda_q2, lambda_k2, subln_g, w_out, norm2_g, w_router, router_bias, w_exp_gu, w_exp_down, w_sh_gu, w_sh_down):
    bsz, seq, d = x.shape
    depth = w_ada.shape[0]
    assert depth == 1 and d == N_HEADS * V_DIM
    tl = _tiles(seq)
    t = bsz * seq
    x2 = x.reshape(t, d)
    pos = positions.astype(F32).reshape(t, 1)
    l = 0

    mod = _adaln(c, w_ada[l], b_ada[l])
    mod8 = jnp.pad(mod.reshape(bsz, 6, d), ((0, 0), (0, 2), (0, 0)))

    proj = _inproj(x2, mod8, norm1_g[l], w_in[l].astype(BF16), seq, tl["inproj"], 1024)

    gain_scale = math.sqrt(HEAD_DIM ** -0.5 * math.log2(math.e))
    qg = q_norm_g[l] * gain_scale
    kg = k_norm_g[l] * gain_scale
    qg_tab = jnp.tile(qg, 2).reshape(1, LANES)
    kg_tab = jnp.tile(kg, 2).reshape(1, LANES)
    seg_id = jnp.arange(LANES) // HEAD_DIM
    seg_mat = (seg_id[:, None] == seg_id[None, :]).astype(BF16)
    lam = (jnp.exp(jnp.sum(lambda_q1[l] * lambda_k1[l])) - jnp.exp(jnp.sum(lambda_q2[l] * lambda_k2[l]))
           + LAMBDA_INIT)
    bound = 1.15 * HEAD_DIM * jnp.max(jnp.abs(qg)) * jnp.max(jnp.abs(kg))
    scal = jnp.stack([lam, bound]).astype(F32).reshape(1, 2)

    def attend(exact):
        qt, kn, vt = _qkvprep(pos, proj, qg_tab, kg_tab, seg_mat, bsz, seq, tl["qk"], BF16 if exact else QK_DTYPE)
        return _attention(scal, qt, kn.reshape(bsz, seq, -1), vt, subln_g[l], bsz, seq, tl["attn"], exact=exact)

    y_attn = lax.cond(2.0 * bound < SAFE_EXP2_RANGE, lambda: attend(False), lambda: attend(True))

    y_conv = _conv_branch(proj, conv_dw[l], conv_dw_b[l], conv_ln_g[l], conv_ln_b[l],
                          w_pw2[l].astype(BF16), b_pw2[l], seq, tl["conv"])

    x1, h2, logits_t, shared = _merge(x2, y_conv, y_attn, proj, mod8, norm2_g[l], w_out[l].astype(BF16),
                                      w_router[l].T, w_sh_gu[l].astype(BF16), w_sh_down[l].astype(BF16),
                                      seq, tl["merge"])
    gates, eid_k, pos_k, gate_k, count = _route(logits_t, router_bias[l], tl["route"])
    wgu_bf = w_exp_gu[l].astype(BF16)
    wd_bf = w_exp_down[l].astype(BF16)
    nwin = min(MOE_WINDOWS_PER_STEP, seq // tl["route"])
    count = count[:, 0].reshape(-1, N_EXPERTS).astype(jnp.int32)
    run_start, fits = _dispatch_tables(count)
    out = lax.cond(fits,
                   lambda: _moe_sparse(h2, x1, shared, eid_k, pos_k, gate_k, run_start, count, mod8,
                                       wgu_bf, wd_bf, seq, tl["route"], nwin, MOE_EXPERT_BLOCK),
                   lambda: _moe(h2, x1, shared, gates, mod8, wgu_bf, wd_bf, seq, tl["moe"], 4))
    return out.reshape(bsz, seq, d)
```
